```python
import math, functools
import jax, jax.numpy as jnp
from jax import lax
import numpy as np

D_MODEL = 4096
BATCH = 4
SEQ = 2048
DEPTH = 2
DEC_BATCH = 128
DEC_SEQ = 8
PAST_LEN = 16384
PAGE_SIZE = 128

GLA_HEADS = 8
GLA_DK = 128
GLA_DV = 256
GLA_GATE_RANK = 16
GLA_GATE_NORM = 16.0
GLA_CHUNK = 16
GLA_W = GLA_HEADS * GLA_DV
MLA_HEADS = 16
MLA_NOPE = 128
MLA_ROPE = 64
MLA_V = 128
Q_LORA = 896
KV_LORA = 512
MLA_W = MLA_HEADS * MLA_V
MLA_SCALE = (MLA_NOPE + MLA_ROPE) ** -0.5
ROPE_BASE = 10000.0
Q_BLOCK = 128
MIX_W = GLA_W + MLA_W
D_FF = -(-8 * D_MODEL // (3 * 256)) * 256
ALPHA = (2.0 * DEPTH) ** 0.25
BETA = (8.0 * DEPTH) ** -0.25
NEG = -1e30
IN_SPLITS = (GLA_HEADS * GLA_DK, GLA_HEADS * GLA_DK, GLA_W, GLA_GATE_RANK, GLA_W, Q_LORA, KV_LORA, MLA_ROPE)
IN_COLS = sum(IN_SPLITS)
SPLIT_IDX = tuple(int(i) for i in np.cumsum(IN_SPLITS)[:-1])

kernel_name = "hymba_gla_mla_deepnorm_step"


def layer_norm(x, g, b, eps=1e-5):
    xf = x.astype(jnp.float32)
    mu = xf.mean(-1, keepdims=True)
    var = jnp.square(xf - mu).mean(-1, keepdims=True)
    return ((xf - mu) * lax.rsqrt(var + eps) * g.astype(jnp.float32) + b.astype(jnp.float32)).astype(x.dtype)


def rms_norm(x, g, eps=1e-6):
    xf = x.astype(jnp.float32)
    return (xf * lax.rsqrt(jnp.square(xf).mean(-1, keepdims=True) + eps) * g.astype(jnp.float32)).astype(x.dtype)


def rope(x, pos):
    half = x.shape[-1] // 2
    inv = ROPE_BASE ** (-jnp.arange(half, dtype=jnp.float32) / half)
    ang = pos.astype(jnp.float32)[:, None] * inv[None, :]
    cos = jnp.cos(ang)[:, None, :]
    sin = jnp.sin(ang)[:, None, :]
    x1 = x[..., :half].astype(jnp.float32)
    x2 = x[..., half:].astype(jnp.float32)
    return jnp.concatenate([x1 * cos - x2 * sin, x2 * cos + x1 * sin], axis=-1).astype(x.dtype)


def gla_recurrence(q, k, v, log_a, s0):
    B, T = q.shape[:2]
    C = GLA_CHUNK if T % GLA_CHUNK == 0 else T
    n = T // C

    def to_chunks(t):
        return jnp.moveaxis(t.astype(jnp.float32).reshape(B, n, C, *t.shape[2:]), 1, 0)

    mask = jnp.tril(jnp.ones((C, C), dtype=bool))

    def step(S, inp):
        qc, kc, vc, ac = inp
        b = jnp.cumsum(ac, axis=1)
        b_last = b[:, -1]
        q_dec = qc * jnp.exp(b)
        k_dec = kc * jnp.exp(-b)
        o_inter = jnp.einsum('bchk,bhkv->bchv', q_dec, S)
        att = jnp.where(mask, jnp.einsum('bihk,bjhk->bhij', q_dec, k_dec), 0.0)
        o_intra = jnp.einsum('bhij,bjhv->bihv', att, vc)
        k_tail = kc * jnp.exp(b_last[:, None] - b)
        S_new = S * jnp.exp(b_last)[..., None] + jnp.einsum('bchk,bchv->bhkv', k_tail, vc)
        return S_new, o_inter + o_intra

    S_fin, o = lax.scan(step, s0.astype(jnp.float32), (to_chunks(q), to_chunks(k), to_chunks(v), to_chunks(log_a)))
    o = jnp.moveaxis(o, 0, 1).reshape(B, T, GLA_HEADS, GLA_DV)
    return o.astype(q.dtype), S_fin.astype(s0.dtype)


def mla_prompt_attention(q_lat, q_pe, ckv, kpe):
    B, S, H, _ = q_lat.shape
    nb = S // Q_BLOCK
    ck = ckv.astype(jnp.float32)
    kp = kpe.astype(jnp.float32)
    kpos = jnp.arange(S)

    def block(i):
        ql = lax.dynamic_slice_in_dim(q_lat, i * Q_BLOCK, Q_BLOCK, axis=1).astype(jnp.float32)
        qp = lax.dynamic_slice_in_dim(q_pe, i * Q_BLOCK, Q_BLOCK, axis=1).astype(jnp.float32)
        s = (jnp.einsum('bqhc,bkc->bhqk', ql, ck) + jnp.einsum('bqhr,bkr->bhqk', qp, kp)) * MLA_SCALE
        qpos = i * Q_BLOCK + jnp.arange(Q_BLOCK)
        s = jnp.where(kpos[None, :] <= qpos[:, None], s, NEG)
        p = jax.nn.softmax(s, axis=-1)
        return jnp.einsum('bhqk,bkc->bqhc', p, ck)

    o = lax.map(block, jnp.arange(nb))
    return jnp.moveaxis(o, 0, 1).reshape(B, S, H, KV_LORA).astype(q_lat.dtype)


def mla_sample_attention(q_lat, q_pe, ckv, kpe, pool_ckv, pool_kpe, page_table):
    B, T, H, _ = q_lat.shape
    ql = q_lat.astype(jnp.float32)
    qp = q_pe.astype(jnp.float32)

    def scores(ck, kp):
        return (jnp.einsum('bqhc,bkc->bhqk', ql, ck) + jnp.einsum('bqhr,bkr->bhqk', qp, kp)) * MLA_SCALE

    def update(carry, s, ck):
        m, l, acc = carry
        m_new = jnp.maximum(m, s.max(-1))
        corr = jnp.exp(m - m_new)
        p = jnp.exp(s - m_new[..., None])
        return (m_new, l * corr + p.sum(-1), acc * corr[..., None] + jnp.einsum('bhqk,bkc->bhqc', p, ck))

    def page_step(carry, phys):
        ck = pool_ckv[phys].astype(jnp.float32)
        kp = pool_kpe[phys].astype(jnp.float32)
        return update(carry, scores(ck, kp), ck), None

    init = (jnp.full((B, H, T), NEG, jnp.float32), jnp.zeros((B, H, T), jnp.float32),
            jnp.zeros((B, H, T, KV_LORA), jnp.float32))
    carry, _ = lax.scan(page_step, init, page_table.T)
    ck = ckv.astype(jnp.float32)
    kp = kpe.astype(jnp.float32)
    causal = jnp.tril(jnp.ones((T, T), dtype=bool))
    s = jnp.where(causal, scores(ck, kp), NEG)
    m, l, acc = update(carry, s, ck)
    o = acc / l[..., None]
    return jnp.transpose(o, (0, 2, 1, 3)).astype(q_lat.dtype)


def hybrid_mixer(x, pos, gla_s0, attend, w_in, w_gk, b_gk, gla_norm_g, q_norm_g, w_uq, kv_norm_g, w_uk, w_uv, w_o):
    B, T, _ = x.shape
    proj = x @ w_in
    gq, gk, gv, g_low, gr, cq, ckv, kpe = jnp.split(proj, SPLIT_IDX, axis=-1)
    q = gq.reshape(B, T, GLA_HEADS, GLA_DK) * (GLA_DK ** -0.5)
    k = gk.reshape(B, T, GLA_HEADS, GLA_DK)
    v = gv.reshape(B, T, GLA_HEADS, GLA_DV)
    log_a = jax.nn.log_sigmoid((g_low @ w_gk + b_gk).astype(jnp.float32)) / GLA_GATE_NORM
    log_a = log_a.reshape(B, T, GLA_HEADS, GLA_DK)
    o_gla, s_fin = gla_recurrence(q, k, v, log_a, gla_s0)
    o_gla = rms_norm(o_gla, gla_norm_g) * jax.nn.silu(gr.reshape(B, T, GLA_HEADS, GLA_DV))
    q_all = jnp.einsum('btc,chd->bthd', rms_norm(cq, q_norm_g), w_uq)
    q_nope = q_all[..., :MLA_NOPE]
    q_pe = rope(q_all[..., MLA_NOPE:], pos)
    ckv = rms_norm(ckv, kv_norm_g)
    kpe = rope(kpe[:, :, None, :], pos)[:, :, 0, :]
    q_lat = jnp.einsum('bthd,chd->bthc', q_nope, w_uk)
    o_lat = attend(q_lat, q_pe, ckv, kpe)
    o_mla = jnp.einsum('bthc,chd->bthd', o_lat, w_uv)
    o = jnp.concatenate([o_gla.reshape(B, T, GLA_W), o_mla.reshape(B, T, MLA_W)], axis=-1)
    return o @ w_o, s_fin, ckv, kpe


def swiglu_ffn(x, w_gate, w_up, w_down):
    return (jax.nn.silu(x @ w_gate) * (x @ w_up)) @ w_down


def decoder_layer(x, pos, gla_s0, attend, w_in, w_gk, b_gk, gla_norm_g, q_norm_g, w_uq, kv_norm_g, w_uk, w_uv,
                  w_o, ln1_g, ln1_b, w_ffn_gate, w_ffn_up, w_ffn_down, ln2_g, ln2_b):
    h, s_fin, ckv, kpe = hybrid_mixer(x, pos, gla_s0, attend, w_in, w_gk, b_gk, gla_norm_g, q_norm_g, w_uq,
                                      kv_norm_g, w_uk, w_uv, w_o)
    x = layer_norm(ALPHA * x + h, ln1_g, ln1_b)
    x = layer_norm(ALPHA * x + swiglu_ffn(x, w_ffn_gate, w_ffn_up, w_ffn_down), ln2_g, ln2_b)
    return x, s_fin, ckv, kpe


def setup_inputs(seed: int = 0) -> dict:
    key = jax.random.key(seed)
    ks = jax.random.split(key, 24)
    n_pages = PAST_LEN // PAGE_SIZE
    n_used = DEC_BATCH * n_pages
    n_pool = n_used + n_used // 4
    nrm = jax.random.normal
    f32 = jnp.float32
    d = {}
    d['x_prompt'] = nrm(ks[0], (BATCH, SEQ, D_MODEL), f32)
    d['x_sample'] = nrm(ks[1], (DEC_BATCH, DEC_SEQ, D_MODEL), f32)
    d['cache_ckv'] = nrm(ks[2], (DEPTH, n_pool, PAGE_SIZE, KV_LORA), f32)
    d['cache_kpe'] = nrm(ks[3], (DEPTH, n_pool, PAGE_SIZE, MLA_ROPE), f32)
    d['state_gla'] = 0.1 * nrm(ks[4], (DEPTH, DEC_BATCH, GLA_HEADS, GLA_DK, GLA_DV), f32)
    d['page_table'] = jax.random.permutation(ks[5], n_pool)[:n_used].reshape(DEC_BATCH, n_pages).astype(jnp.int32)
    d['w_in'] = nrm(ks[6], (DEPTH, D_MODEL, IN_COLS), f32) * D_MODEL ** -0.5
    d['w_gk'] = nrm(ks[7], (DEPTH, GLA_GATE_RANK, GLA_HEADS * GLA_DK), f32) * GLA_GATE_RANK ** -0.5
    d['b_gk'] = 0.01 * nrm(ks[8], (DEPTH, GLA_HEADS * GLA_DK), f32)
    d['gla_norm_g'] = 1.0 + 0.01 * nrm(ks[9], (DEPTH, GLA_DV), f32)
    d['q_norm_g'] = 1.0 + 0.01 * nrm(ks[10], (DEPTH, Q_LORA), f32)
    d['w_uq'] = nrm(ks[11], (DEPTH, Q_LORA, MLA_HEADS, MLA_NOPE + MLA_ROPE), f32) * Q_LORA ** -0.5
    d['kv_norm_g'] = 1.0 + 0.01 * nrm(ks[12], (DEPTH, KV_LORA), f32)
    d['w_uk'] = nrm(ks[13], (DEPTH, KV_LORA, MLA_HEADS, MLA_NOPE), f32) * KV_LORA ** -0.5
    d['w_uv'] = nrm(ks[14], (DEPTH, KV_LORA, MLA_HEADS, MLA_V), f32) * KV_LORA ** -0.5
    d['w_o'] = nrm(ks[15], (DEPTH, MIX_W, D_MODEL), f32) * (MIX_W ** -0.5 * BETA)
    d['ln1_g'] = 1.0 + 0.01 * nrm(ks[16], (DEPTH, D_MODEL), f32)
    d['ln1_b'] = 0.01 * nrm(ks[17], (DEPTH, D_MODEL), f32)
    d['w_ffn_gate'] = nrm(ks[18], (DEPTH, D_MODEL, D_FF), f32) * D_MODEL ** -0.5
    d['w_ffn_up'] = nrm(ks[19], (DEPTH, D_MODEL, D_FF), f32) * D_MODEL ** -0.5
    d['w_ffn_down'] = nrm(ks[20], (DEPTH, D_FF, D_MODEL), f32) * (D_FF ** -0.5 * BETA)
    d['ln2_g'] = 1.0 + 0.01 * nrm(ks[21], (DEPTH, D_MODEL), f32)
    d['ln2_b'] = 0.01 * nrm(ks[22], (DEPTH, D_MODEL), f32)
    return d


def reference(x_prompt, x_sample, cache_ckv, cache_kpe, state_gla, page_table, w_in, w_gk, b_gk, gla_norm_g,
              q_norm_g, w_uq, kv_norm_g, w_uk, w_uv, w_o, ln1_g, ln1_b, w_ffn_gate, w_ffn_up, w_ffn_down,
              ln2_g, ln2_b):
    pos_p = jnp.arange(SEQ, dtype=jnp.int32)
    pos_s = PAST_LEN + jnp.arange(DEC_SEQ, dtype=jnp.int32)
    xp, xs = x_prompt, x_sample
    ckv_p_l, kpe_p_l, gla_p_l, ckv_s_l, kpe_s_l, gla_s_l = [], [], [], [], [], []
    for l in range(DEPTH):
        lw = (w_in[l], w_gk[l], b_gk[l], gla_norm_g[l], q_norm_g[l], w_uq[l], kv_norm_g[l], w_uk[l], w_uv[l],
              w_o[l], ln1_g[l], ln1_b[l], w_ffn_gate[l], w_ffn_up[l], w_ffn_down[l], ln2_g[l], ln2_b[l])
        s0_p = jnp.zeros((BATCH, GLA_HEADS, GLA_DK, GLA_DV), state_gla.dtype)
        xp, s_p, ckv_p, kpe_p = decoder_layer(xp, pos_p, s0_p, mla_prompt_attention, *lw)
        attend_s = functools.partial(mla_sample_attention, pool_ckv=cache_ckv[l], pool_kpe=cache_kpe[l],
                                     page_table=page_table)
        xs, s_s, ckv_s, kpe_s = decoder_layer(xs, pos_s, state_gla[l], attend_s, *lw)
        ckv_p_l.append(ckv_p); kpe_p_l.append(kpe_p); gla_p_l.append(s_p)
        ckv_s_l.append(ckv_s); kpe_s_l.append(kpe_s); gla_s_l.append(s_s)
    new_ckv_prompt = jnp.stack(ckv_p_l)
    new_kpe_prompt = jnp.stack(kpe_p_l)
    new_gla_prompt = jnp.stack(gla_p_l)
    new_ckv_sample = jnp.stack(ckv_s_l)
    new_kpe_sample = jnp.stack(kpe_s_l)
    new_gla_sample = jnp.stack(gla_s_l)
    return (xp, xs, new_ckv_prompt, new_kpe_prompt, new_gla_prompt, new_ckv_sample, new_kpe_sample, new_gla_sample)
```

```python
import functools

import jax
import jax.numpy as jnp
from jax import lax
from jax.experimental import pallas as pl
from jax.experimental.pallas import tpu as pltpu

GLA_HEADS = 8
GLA_DK = 128
GLA_DV = 256
GLA_GATE_RANK = 16
GLA_GATE_NORM = 16.0
MLA_HEADS = 16
MLA_NOPE = 128
MLA_ROPE = 64
MLA_V = 128
Q_LORA = 896
KV_LORA = 512
MLA_SCALE = (MLA_NOPE + MLA_ROPE) ** -0.5
ROPE_BASE = 10000.0
PAGE_SIZE = 128
NEG = -1e30
LN_EPS = 1e-5
RMS_EPS = 1e-6

LANES = 128
GLA_SUB = 32
GLA_TILE = 128
QK_PAD = KV_LORA + LANES
PAGES_PER_STEP = 8
VMEM_LIMIT = 56 * 1024 * 1024

P_CQ = 0
P_CKV = 896
P_KPE = 1408
P_GLOW = 1536
P_HEAD = 2048
P_GV = 2048
P_GR = 4096
P_GQ = 6144
P_GK = 7168
P_COLS = 8192

BF16 = jnp.bfloat16
F32 = jnp.float32


def _cparams(n_axes):
    return pltpu.CompilerParams(dimension_semantics=("arbitrary",) * n_axes,
                                vmem_limit_bytes=VMEM_LIMIT)


def _pick(n, pref):
    if n <= pref:
        return n
    t = pref
    while n % t:
        t //= 2
    return t


def _bdot(a, b):
    return jnp.dot(a.astype(BF16), b.astype(BF16), preferred_element_type=F32)


def _bdot_nt(a, b):
    return lax.dot_general(a.astype(BF16), b.astype(BF16), (((1,), (1,)), ((), ())),
                           preferred_element_type=F32)


def _mm_kernel(x_ref, w_ref, o_ref):
    o_ref[...] = jnp.dot(x_ref[...], w_ref[...], preferred_element_type=F32).astype(o_ref.dtype)


def _matmul(x, w, out_dtype, tm, tn, name):
    m, k = x.shape
    n = w.shape[1]
    tm, tn = _pick(m, tm), _pick(n, tn)
    return pl.pallas_call(
        _mm_kernel,
        grid=(m // tm, n // tn),
        in_specs=[pl.BlockSpec((tm, k), lambda i, j: (i, 0)),
                  pl.BlockSpec((k, tn), lambda i, j: (0, j))],
        out_specs=pl.BlockSpec((tm, tn), lambda i, j: (i, j)),
        out_shape=jax.ShapeDtypeStruct((m, n), out_dtype),
        compiler_params=_cparams(2),
        name=name,
    )(x, w)


def _mm2_kernel(a_ref, b_ref, wa_ref, wb_ref, o_ref):
    acc = jnp.dot(a_ref[...], wa_ref[...], preferred_element_type=F32)
    acc = acc + jnp.dot(b_ref[...], wb_ref[...], preferred_element_type=F32)
    o_ref[...] = acc


def _matmul2(a, b, wa, wb, tm, tn, name):
    m, ka = a.shape
    kb = b.shape[1]
    n = wa.shape[1]
    tm, tn = _pick(m, tm), _pick(n, tn)
    return pl.pallas_call(
        _mm2_kernel,
        grid=(m // tm, n // tn),
        in_specs=[pl.BlockSpec((tm, ka), lambda i, j: (i, 0)),
                  pl.BlockSpec((tm, kb), lambda i, j: (i, 0)),
                  pl.BlockSpec((ka, tn), lambda i, j: (0, j)),
                  pl.BlockSpec((kb, tn), lambda i, j: (0, j))],
        out_specs=pl.BlockSpec((tm, tn), lambda i, j: (i, j)),
        out_shape=jax.ShapeDtypeStruct((m, n), F32),
        compiler_params=_cparams(2),
        name=name,
    )(a, b, wa, wb)


def _swiglu_kernel(x_ref, w_ref, o_ref, *, tf):
    gu = jnp.dot(x_ref[...], w_ref[...], preferred_element_type=F32)
    g = gu[:, :tf]
    u = gu[:, tf:]
    o_ref[...] = (g * (1.0 / (1.0 + jnp.exp(-g))) * u).astype(o_ref.dtype)


def _swiglu_up(x, w_gu, d_ff, tf, tm, name):
    m, k = x.shape
    tm = _pick(m, tm)
    return pl.pallas_call(
        functools.partial(_swiglu_kernel, tf=tf),
        grid=(m // tm, d_ff // tf),
        in_specs=[pl.BlockSpec((tm, k), lambda i, j: (i, 0)),
                  pl.BlockSpec((k, 2 * tf), lambda i, j: (0, j))],
        out_specs=pl.BlockSpec((tm, tf), lambda i, j: (i, j)),
        out_shape=jax.ShapeDtypeStruct((m, d_ff), BF16),
        compiler_params=_cparams(2),
        name=name,
    )(x, w_gu)


def _ln_kernel(x_ref, h_ref, g_ref, b_ref, o_ref, ob_ref, *, alpha):
    y = alpha * x_ref[...] + h_ref[...]
    mu = jnp.mean(y, axis=-1, keepdims=True)
    d = y - mu
    var = jnp.mean(d * d, axis=-1, keepdims=True)
    o = d * lax.rsqrt(var + LN_EPS) * g_ref[...] + b_ref[...]
    o_ref[...] = o
    ob_ref[...] = o.astype(BF16)


def _res_ln(x, h, g, b, alpha, row0, rows, name):
    d = x.shape[1]
    tm = _pick(rows, 256)
    off = row0 // tm
    return pl.pallas_call(
        functools.partial(_ln_kernel, alpha=alpha),
        grid=(rows // tm,),
        in_specs=[pl.BlockSpec((tm, d), lambda i: (i + off, 0)),
                  pl.BlockSpec((tm, d), lambda i: (i + off, 0)),
                  pl.BlockSpec((1, d), lambda i: (0, 0)),
                  pl.BlockSpec((1, d), lambda i: (0, 0))],
        out_specs=[pl.BlockSpec((tm, d), lambda i: (i, 0)),
                   pl.BlockSpec((tm, d), lambda i: (i, 0))],
        out_shape=[jax.ShapeDtypeStruct((rows, d), F32),
                   jax.ShapeDtypeStruct((rows, d), BF16)],
        compiler_params=_cparams(1),
        name=name,
    )(x, h, g.reshape(1, d), b.reshape(1, d))


def _log_sigmoid(z):
    return jnp.minimum(z, 0.0) - jnp.log(1.0 + jnp.exp(-jnp.abs(z)))


def _split3(x):
    hi = x.astype(BF16)
    r1 = x - hi.astype(F32)
    mid = r1.astype(BF16)
    lo = (r1 - mid.astype(F32)).astype(BF16)
    return jnp.concatenate([hi, mid, lo], axis=1)


def _sum3(y, w):
    return y[:, :w] + y[:, w:2 * w] + y[:, 2 * w:3 * w]


def _iota2(shape):
    return lax.broadcasted_iota(jnp.int32, shape, 0), lax.broadcasted_iota(jnp.int32, shape, 1)


def _idiv(x, n):
    return x >> (n.bit_length() - 1) if n & (n - 1) == 0 else x // n


def _imod(x, n):
    return x & (n - 1) if n & (n - 1) == 0 else x % n


def _ones_where(mask):
    return jnp.where(mask, 1.0, 0.0).astype(BF16)


def _log_gate(gl, wgk, bgk):
    z = jnp.dot(gl.astype(BF16), wgk, preferred_element_type=F32) + bgk
    return _log_sigmoid(z) / GLA_GATE_NORM


def _gla_finish(o, gn, gr):
    ms = jnp.mean(o * o, axis=-1, keepdims=True)
    on = o * lax.rsqrt(ms + RMS_EPS) * gn
    return on * (gr * (1.0 / (1.0 + jnp.exp(-gr))))


def _rows_bcast(x, row, n):
    return jnp.broadcast_to(x[row:row + 1, :], (n, x.shape[1]))


def _gla_prompt_kernel(q_ref, k_ref, v_ref, gl_ref, gr_ref, wgk_ref, bgk_ref, gn_ref,
                       o_ref, sfin_ref, st_ref, *, n_tiles):
    t_blk = pl.program_id(2)

    @pl.when(t_blk == 0)
    def _():
        st_ref[...] = jnp.zeros_like(st_ref)

    n = GLA_TILE
    n_sub = n // GLA_SUB
    r, c = _iota2((n, n))
    tri = _ones_where(c <= r)
    diag_mask = (_idiv(r, GLA_SUB) == _idiv(c, GLA_SUB)) & (c <= r)
    c_sub = lax.broadcasted_iota(jnp.int32, (GLA_SUB, n), 1)
    scale = GLA_DK ** -0.5

    for t in range(n_tiles):
        rows = pl.ds(t * n, n)
        qs = q_ref[rows, :] * scale
        k = k_ref[rows, :]
        v = v_ref[rows, :]
        la = _log_gate(gl_ref[rows, :], wgk_ref[...], bgk_ref[...])
        g = _sum3(jnp.dot(tri, _split3(la), preferred_element_type=F32), GLA_DK)
        g_last = g[n - 1:n, :]

        g_mid = jnp.concatenate(
            [_rows_bcast(g, s * GLA_SUB + GLA_SUB // 2 - 1, GLA_SUB) for s in range(n_sub)], axis=0)
        g_start = jnp.concatenate(
            [jnp.zeros((GLA_SUB, GLA_DK), F32)]
            + [_rows_bcast(g, s * GLA_SUB - 1, GLA_SUB) for s in range(1, n_sub)], axis=0)

        att = jnp.where(diag_mask, _bdot_nt(qs * jnp.exp(g - g_mid), k * jnp.exp(g_mid - g)), 0.0)
        q_off = qs * jnp.exp(g - g_start)
        blocks = [att[:GLA_SUB, :]]
        for s in range(1, n_sub):
            k_off = k * jnp.exp(jnp.minimum(g[s * GLA_SUB - 1:s * GLA_SUB, :] - g, 0.0))
            a = _bdot_nt(q_off[s * GLA_SUB:(s + 1) * GLA_SUB, :], k_off)
            blocks.append(jnp.where(c_sub < s * GLA_SUB, a, att[s * GLA_SUB:(s + 1) * GLA_SUB, :]))
        att = jnp.concatenate(blocks, axis=0)

        st = st_ref[...]
        o = _bdot(att, v) + _bdot_nt(qs * jnp.exp(g), st)
        k_tail = k * jnp.exp(g_last - g)
        st_ref[...] = st * jnp.exp(g_last) + _bdot(v.T, k_tail)
        o_ref[rows, :] = _gla_finish(o, gn_ref[...], gr_ref[rows, :]).astype(o_ref.dtype)

    @pl.when(t_blk == pl.num_programs(2) - 1)
    def _():
        sfin_ref[...] = st_ref[...].T


def _gla_prompt(p, wgk, bgk, gn, batch, seq, name):
    ct = _pick(seq, 512)
    n_t = seq // ct
    w128 = lambda base, h: base // GLA_DK + h
    w256 = lambda base, h: base // GLA_DV + h
    row = lambda b, h, t: b * n_t + t
    return pl.pallas_call(
        functools.partial(_gla_prompt_kernel, n_tiles=ct // GLA_TILE),
        grid=(batch, GLA_HEADS, n_t),
        in_specs=[pl.BlockSpec((ct, GLA_DK), lambda b, h, t: (row(b, h, t), w128(P_GQ, h))),
                  pl.BlockSpec((ct, GLA_DK), lambda b, h, t: (row(b, h, t), w128(P_GK, h))),
                  pl.BlockSpec((ct, GLA_DV), lambda b, h, t: (row(b, h, t), w256(P_GV, h))),
                  pl.BlockSpec((ct, LANES), lambda b, h, t: (row(b, h, t), P_GLOW // LANES)),
                  pl.BlockSpec((ct, GLA_DV), lambda b, h, t: (row(b, h, t), w256(P_GR, h))),
                  pl.BlockSpec((LANES, GLA_DK), lambda b, h, t: (0, h)),
                  pl.BlockSpec((1, GLA_DK), lambda b, h, t: (0, h)),
                  pl.BlockSpec((1, GLA_DV), lambda b, h, t: (0, 0))],
        out_specs=[pl.BlockSpec((ct, GLA_DV), lambda b, h, t: (row(b, h, t), h)),
                   pl.BlockSpec((None, None, GLA_DK, GLA_DV), lambda b, h, t: (b, h, 0, 0))],
        out_shape=[jax.ShapeDtypeStruct((batch * seq, GLA_HEADS * GLA_DV), BF16),
                   jax.ShapeDtypeStruct((batch, GLA_HEADS, GLA_DK, GLA_DV), F32)],
        scratch_shapes=[pltpu.VMEM((GLA_DV, GLA_DK), F32)],
        compiler_params=_cparams(3),
        name=name,
    )(p, p, p, p, p, wgk, bgk, gn)


def _gla_sample_kernel(q_ref, k_ref, v_ref, gl_ref, gr_ref, wgk_ref, bgk_ref, gn_ref, s0_ref,
                       o_ref, sfin_ref, oi_ref, bl_ref, kt_ref, vt_ref, qd_ref, *, t_new, n_b):
    n = n_b * t_new
    r, c = _iota2((n, n))
    same = _idiv(r, t_new) == _idiv(c, t_new)
    causal = same & (c <= r)
    tri = _ones_where(causal)
    ones = _ones_where(same)
    scale = GLA_DK ** -0.5

    qs = q_ref[...] * scale
    k = k_ref[...]
    v = v_ref[...]
    la = _log_gate(gl_ref[...], wgk_ref[...], bgk_ref[...])
    la3 = _split3(la)
    b = _sum3(jnp.dot(tri, la3, preferred_element_type=F32), GLA_DK)
    b_last = _sum3(jnp.dot(ones, la3, preferred_element_type=F32), GLA_DK)
    q_dec = qs * jnp.exp(b)
    att = jnp.where(causal, _bdot_nt(q_dec, k * jnp.exp(-b)), 0.0)
    o_intra = _bdot(att, v)

    bl_ref[...] = b_last
    kt_ref[...] = (k * jnp.exp(b_last - b)).astype(BF16)
    vt_ref[...] = v.T.astype(BF16)
    qd_ref[...] = q_dec.astype(BF16)
    oi_ref[...] = jnp.zeros_like(oi_ref)
    row_seq = _idiv(lax.broadcasted_iota(jnp.int32, (n, 1), 0), t_new)

    def body(i, carry):
        st = s0_ref[i].T
        mine = row_seq == i
        o_full = lax.dot_general(qd_ref[...], st.astype(BF16), (((1,), (1,)), ((), ())),
                                 preferred_element_type=F32)
        oi_ref[...] += jnp.where(mine, o_full, 0.0)
        decay = jnp.exp(bl_ref[pl.ds(pl.multiple_of(i * t_new, t_new), t_new), :][0:1, :])
        kt = jnp.where(mine, kt_ref[...], jnp.zeros_like(kt_ref))
        st_new = st * decay + jnp.dot(vt_ref[...], kt, preferred_element_type=F32)
        sfin_ref[i] = st_new.T
        return carry

    lax.fori_loop(0, n_b, body, 0)
    o_ref[...] = _gla_finish(oi_ref[...] + o_intra, gn_ref[...], gr_ref[...]).astype(o_ref.dtype)


def _gla_sample(p, wgk, bgk, gn, s0, layer, row0, dec_b, t_new, name):
    n_b = GLA_TILE // t_new
    n = n_b * t_new
    r0 = row0 // n
    w128 = lambda base, h: base // GLA_DK + h
    w256 = lambda base, h: base // GLA_DV + h
    return pl.pallas_call(
        functools.partial(_gla_sample_kernel, t_new=t_new, n_b=n_b),
        grid=(dec_b // n_b, GLA_HEADS),
        in_specs=[pl.BlockSpec((n, GLA_DK), lambda i, h: (r0 + i, w128(P_GQ, h))),
                  pl.BlockSpec((n, GLA_DK), lambda i, h: (r0 + i, w128(P_GK, h))),
                  pl.BlockSpec((n, GLA_DV), lambda i, h: (r0 + i, w256(P_GV, h))),
                  pl.BlockSpec((n, LANES), lambda i, h: (r0 + i, P_GLOW // LANES)),
                  pl.BlockSpec((n, GLA_DV), lambda i, h: (r0 + i, w256(P_GR, h))),
                  pl.BlockSpec((LANES, GLA_DK), lambda i, h: (0, h)),
                  pl.BlockSpec((1, GLA_DK), lambda i, h: (0, h)),
                  pl.BlockSpec((1, GLA_DV), lambda i, h: (0, 0)),
                  pl.BlockSpec((None, n_b, None, GLA_DK, GLA_DV), lambda i, h: (layer, i, h, 0, 0))],
        out_specs=[pl.BlockSpec((n, GLA_DV), lambda i, h: (i, h)),
                   pl.BlockSpec((n_b, None, GLA_DK, GLA_DV), lambda i, h: (i, h, 0, 0))],
        out_shape=[jax.ShapeDtypeStruct((dec_b * t_new, GLA_HEADS * GLA_DV), BF16),
                   jax.ShapeDtypeStruct(s0.shape[1:], F32)],
        scratch_shapes=[pltpu.VMEM((n, GLA_DV), F32),
                        pltpu.VMEM((n, GLA_DK), F32),
                        pltpu.VMEM((n, GLA_DK), BF16),
                        pltpu.VMEM((GLA_DV, n), BF16),
                        pltpu.VMEM((n, GLA_DK), BF16)],
        compiler_params=_cparams(2),
        name=name,
    )(p, p, p, p, p, wgk, bgk, gn, s0)


def _rms(x, g):
    return x * lax.rsqrt(jnp.mean(x * x, axis=-1, keepdims=True) + RMS_EPS) * g


def _mla_prep_kernel(p_ref, qg_ref, kg_ref, wq_ref, wuk_ref, cos_ref, sin_ref,
                     q_ref, ckv_ref, kpe_ref, kcat_ref):
    nope_w = MLA_HEADS * MLA_NOPE
    rope_w = MLA_HEADS * MLA_ROPE
    cq = p_ref[:, P_CQ:P_CQ + Q_LORA]
    qa = jnp.dot(_rms(cq, qg_ref[...]).astype(BF16), wq_ref[...], preferred_element_type=F32)
    cos = cos_ref[...]
    sin = sin_ref[...]
    low = lax.broadcasted_iota(jnp.int32, cos.shape, 1) < MLA_ROPE
    for j in range(MLA_HEADS // 2):
        x = qa[:, nope_w + j * LANES:nope_w + (j + 1) * LANES]
        xs = qa[:, nope_w + rope_w + j * LANES:nope_w + rope_w + (j + 1) * LANES]
        rot = x * cos + xs * sin
        tails = (jnp.where(low, rot, 0.0), jnp.where(low, pltpu.roll(rot, MLA_ROPE, 1), 0.0))
        for u in range(2):
            h = 2 * j + u
            q_lat = jnp.dot(qa[:, h * MLA_NOPE:(h + 1) * MLA_NOPE].astype(BF16), wuk_ref[h],
                            preferred_element_type=F32)
            q_ref[h, :, :KV_LORA] = q_lat.astype(BF16)
            q_ref[h, :, KV_LORA:] = tails[u].astype(BF16)

    ckv = _rms(p_ref[:, P_CKV:P_CKV + KV_LORA], kg_ref[...])
    kk = p_ref[:, P_KPE:P_KPE + LANES] * jnp.where(low, cos, sin)
    kpe = kk + pltpu.roll(kk, MLA_ROPE, 1)
    ckv_ref[...] = ckv
    kpe_ref[...] = kpe[:, :MLA_ROPE]
    kcat_ref[:, :KV_LORA] = ckv.astype(BF16)
    kcat_ref[:, KV_LORA:] = jnp.where(low, kpe, 0.0).astype(BF16)


def _mla_prep(p, qg, kg, wq, wuk, cos, sin, name):
    t = p.shape[0]
    tm = _pick(t, 256)
    return pl.pallas_call(
        _mla_prep_kernel,
        grid=(t // tm,),
        in_specs=[pl.BlockSpec((tm, P_HEAD), lambda i: (i, 0)),
                  pl.BlockSpec((1, Q_LORA), lambda i: (0, 0)),
                  pl.BlockSpec((1, KV_LORA), lambda i: (0, 0)),
                  pl.BlockSpec(wq.shape, lambda i: (0, 0)),
                  pl.BlockSpec(wuk.shape, lambda i: (0, 0, 0)),
                  pl.BlockSpec((tm, LANES), lambda i: (i, 0)),
                  pl.BlockSpec((tm, LANES), lambda i: (i, 0))],
        out_specs=[pl.BlockSpec((MLA_HEADS, tm, QK_PAD), lambda i: (0, i, 0)),
                   pl.BlockSpec((tm, KV_LORA), lambda i: (i, 0)),
                   pl.BlockSpec((tm, MLA_ROPE), lambda i: (i, 0)),
                   pl.BlockSpec((tm, QK_PAD), lambda i: (i, 0))],
        out_shape=[jax.ShapeDtypeStruct((MLA_HEADS, t, QK_PAD), BF16),
                   jax.ShapeDtypeStruct((t, KV_LORA), F32),
                   jax.ShapeDtypeStruct((t, MLA_ROPE), F32),
                   jax.ShapeDtypeStruct((t, QK_PAD), BF16)],
        compiler_params=_cparams(1),
        name=name,
    )(p, qg.reshape(1, -1), kg.reshape(1, -1), wq, wuk, cos, sin)


def _attn_prompt_kernel(q_ref, k_ref, wuv_ref, o_ref, m_ref, l_ref, acc_ref, *, tq, tk):
    qi = pl.program_id(1)
    ki = pl.program_id(2)
    last_k = (qi * tq + tq - 1) // tk
    rows = MLA_HEADS * tq

    @pl.when(ki == 0)
    def _():
        m_ref[...] = jnp.full_like(m_ref, NEG)
        l_ref[...] = jnp.zeros_like(l_ref)
        acc_ref[...] = jnp.zeros_like(acc_ref)

    def step(masked):
        q = q_ref[...].reshape(rows, QK_PAD)
        k = k_ref[...]
        s = lax.dot_general(q, k, (((1,), (1,)), ((), ())), preferred_element_type=F32) * MLA_SCALE
        if masked:
            r, c = _iota2((rows, tk))
            s = jnp.where(ki * tk + c <= qi * tq + _imod(r, tq), s, NEG)
        m_old = m_ref[...]
        m_new = jnp.maximum(m_old, jnp.max(s, axis=-1, keepdims=True))
        corr = jnp.exp(m_old - m_new)
        pr = jnp.exp(s - m_new)
        l_ref[...] = l_ref[...] * corr + jnp.sum(pr, axis=-1, keepdims=True)
        acc_ref[...] = acc_ref[...] * corr + jnp.dot(pr.astype(BF16), k[:, :KV_LORA],
                                                     preferred_element_type=F32)
        m_ref[...] = m_new

    @pl.when((ki * tk + tk - 1 <= qi * tq) & (ki <= last_k))
    def _():
        step(False)

    @pl.when((ki * tk + tk - 1 > qi * tq) & (ki <= last_k))
    def _():
        step(True)

    @pl.when(ki == last_k)
    def _():
        o = acc_ref[...] / l_ref[...]
        for h in range(MLA_HEADS):
            o_ref[:, h * MLA_V:(h + 1) * MLA_V] = jnp.dot(
                o[h * tq:(h + 1) * tq, :].astype(BF16), wuv_ref[h],
                preferred_element_type=F32).astype(o_ref.dtype)


def _attn_prompt(q, kcat, wuv, batch, seq, name):
    tq = PAGE_SIZE
    tk = _pick(seq, 512)
    nq, nk = seq // tq, seq // tk

    def k_map(b, qi, ki):
        return (b * nk + jnp.minimum(ki, (qi * tq + tq - 1) // tk), 0)

    return pl.pallas_call(
        functools.partial(_attn_prompt_kernel, tq=tq, tk=tk),
        grid=(batch, nq, nk),
        in_specs=[pl.BlockSpec((MLA_HEADS, tq, QK_PAD), lambda b, qi, ki: (0, b * nq + qi, 0)),
                  pl.BlockSpec((tk, QK_PAD), k_map),
                  pl.BlockSpec(wuv.shape, lambda b, qi, ki: (0, 0, 0))],
        out_specs=pl.BlockSpec((tq, MLA_HEADS * MLA_V), lambda b, qi, ki: (b * nq + qi, 0)),
        out_shape=jax.ShapeDtypeStruct((batch * seq, MLA_HEADS * MLA_V), BF16),
        scratch_shapes=[pltpu.VMEM((MLA_HEADS * tq, 1), F32),
                        pltpu.VMEM((MLA_HEADS * tq, 1), F32),
                        pltpu.VMEM((MLA_HEADS * tq, KV_LORA), F32)],
        compiler_params=_cparams(3),
        name=name,
    )(q, kcat, wuv)


def _attn_sample_kernel(pt_ref, q_ref, *refs, t_new, n_pg):
    ck_refs = refs[:n_pg]
    kp_refs = refs[n_pg:2 * n_pg]
    cn_ref, kn_ref, o_ref, kc_ref, m_ref, l_ref, acc_ref = refs[2 * n_pg:]
    g = pl.program_id(1)
    rows = MLA_HEADS * t_new

    @pl.when(g == 0)
    def _():
        m_ref[...] = jnp.full_like(m_ref, NEG)
        l_ref[...] = jnp.zeros_like(l_ref)
        acc_ref[...] = jnp.zeros_like(acc_ref)
        kc_ref[...] = jnp.zeros_like(kc_ref)

    def update(s, kv):
        m_old = m_ref[...]
        m_new = jnp.maximum(m_old, jnp.max(s, axis=-1, keepdims=True))
        corr = jnp.exp(m_old - m_new)
        pr = jnp.exp(s - m_new)
        l_ref[...] = l_ref[...] * corr + jnp.sum(pr, axis=-1, keepdims=True)
        acc_ref[...] = acc_ref[...] * corr + jnp.dot(pr.astype(BF16), kv, preferred_element_type=F32)
        m_ref[...] = m_new

    q = q_ref[...]
    for i in range(n_pg):
        kc_ref[i * PAGE_SIZE:(i + 1) * PAGE_SIZE, :KV_LORA] = ck_refs[i][...].astype(BF16)
        kc_ref[i * PAGE_SIZE:(i + 1) * PAGE_SIZE, KV_LORA:KV_LORA + MLA_ROPE] = kp_refs[i][...].astype(BF16)
    kc = kc_ref[...]
    s = lax.dot_general(q, kc, (((1,), (1,)), ((), ())), preferred_element_type=F32) * MLA_SCALE
    update(s, kc[:, :KV_LORA])

    @pl.when(g == pl.num_programs(1) - 1)
    def _():
        zc = jnp.zeros((PAGE_SIZE - t_new, KV_LORA), F32)
        zp = jnp.zeros((PAGE_SIZE - t_new, MLA_ROPE), F32)
        kc_ref[:PAGE_SIZE, :KV_LORA] = jnp.concatenate([cn_ref[...], zc], axis=0).astype(BF16)
        kc_ref[:PAGE_SIZE, KV_LORA:KV_LORA + MLA_ROPE] = jnp.concatenate([kn_ref[...], zp], axis=0).astype(BF16)
        kn = kc_ref[:PAGE_SIZE, :]
        sn = lax.dot_general(q, kn, (((1,), (1,)), ((), ())), preferred_element_type=F32) * MLA_SCALE
        r, c = _iota2((rows, PAGE_SIZE))
        sn = jnp.where(c <= _imod(r, t_new), sn, NEG)
        update(sn, kn[:, :KV_LORA])
        o_ref[...] = (acc_ref[...] / l_ref[...]).astype(o_ref.dtype)


def _attn_sample(page_table, q, cache_ckv, cache_kpe, ckv_new, kpe_new, layer, name):
    dec_b, rows, _ = q.shape
    t_new = rows // MLA_HEADS
    n_pages = page_table.shape[1]
    n_pg = _pick(n_pages, PAGES_PER_STEP)

    def page(i):
        return lambda b, g, pt: (layer, pt[b, g * n_pg + i], 0, 0)

    in_specs = [pl.BlockSpec((None, rows, QK_PAD), lambda b, g, pt: (b, 0, 0))]
    in_specs += [pl.BlockSpec((None, None, PAGE_SIZE, KV_LORA), page(i)) for i in range(n_pg)]
    in_specs += [pl.BlockSpec((None, None, PAGE_SIZE, MLA_ROPE), page(i)) for i in range(n_pg)]
    in_specs += [pl.BlockSpec((None, t_new, KV_LORA), lambda b, g, pt: (b, 0, 0)),
                 pl.BlockSpec((None, t_new, MLA_ROPE), lambda b, g, pt: (b, 0, 0))]
    return pl.pallas_call(
        functools.partial(_attn_sample_kernel, t_new=t_new, n_pg=n_pg),
        grid_spec=pltpu.PrefetchScalarGridSpec(
            num_scalar_prefetch=1,
            grid=(dec_b, n_pages // n_pg),
            in_specs=in_specs,
            out_specs=pl.BlockSpec((None, rows, KV_LORA), lambda b, g, pt: (b, 0, 0)),
            scratch_shapes=[pltpu.VMEM((n_pg * PAGE_SIZE, QK_PAD), BF16),
                            pltpu.VMEM((rows, 1), F32),
                            pltpu.VMEM((rows, 1), F32),
                            pltpu.VMEM((rows, KV_LORA), F32)]),
        out_shape=jax.ShapeDtypeStruct((dec_b, rows, KV_LORA), BF16),
        compiler_params=_cparams(2),
        name=name,
    )(page_table, q, *([cache_ckv] * n_pg), *([cache_kpe] * n_pg), ckv_new, kpe_new)


def _head_out_kernel(x_ref, w_ref, o_ref):
    o_ref[...] = jnp.dot(x_ref[...], w_ref[...], preferred_element_type=F32).astype(o_ref.dtype)


def _head_out(x, wuv, name):
    h, t, c = x.shape
    return pl.pallas_call(
        _head_out_kernel,
        grid=(h,),
        in_specs=[pl.BlockSpec((None, t, c), lambda i: (i, 0, 0)),
                  pl.BlockSpec((None, c, MLA_V), lambda i: (i, 0, 0))],
        out_specs=pl.BlockSpec((t, MLA_V), lambda i: (0, i)),
        out_shape=jax.ShapeDtypeStruct((t, h * MLA_V), BF16),
        compiler_params=_cparams(1),
        name=name,
    )(x, wuv)


def _prep_weights(w_in, w_gk, b_gk, w_uq, w_uk, w_uv, w_o, w_gate, w_up, w_down, tf):
    d = w_in.shape[0]
    s_gq, s_gk, s_gv, s_gl, s_gr, s_cq, s_ckv, s_kpe = jnp.split(
        w_in, [1024, 2048, 4096, 4112, 6160, 7056, 7568], axis=1)
    half = MLA_ROPE // 2
    kpe_sw = jnp.concatenate([s_kpe[:, half:], s_kpe[:, :half]], axis=1)
    head = jnp.concatenate([s_cq, s_ckv, s_kpe, kpe_sw, s_gl], axis=1)
    head = jnp.pad(head, ((0, 0), (0, P_HEAD - head.shape[1])))
    wp = jnp.concatenate([head, s_gv, s_gr, s_gq, s_gk], axis=1).astype(BF16)

    wgk = jnp.pad(w_gk, ((0, LANES - GLA_GATE_RANK), (0, 0))).astype(BF16)
    nope = w_uq[:, :, :MLA_NOPE].reshape(Q_LORA, -1)
    pe = w_uq[:, :, MLA_NOPE:]
    pe_sw = jnp.concatenate([pe[:, :, half:], pe[:, :, :half]], axis=2)
    wq = jnp.concatenate([nope, pe.reshape(Q_LORA, -1), pe_sw.reshape(Q_LORA, -1)], axis=1).astype(BF16)
    wuk = jnp.transpose(w_uk, (1, 2, 0)).astype(BF16)
    wuv = jnp.transpose(w_uv, (1, 0, 2)).astype(BF16)
    gla_w = GLA_HEADS * GLA_DV
    wo_a = w_o[:gla_w].astype(BF16)
    wo_b = w_o[gla_w:].astype(BF16)
    d_ff = w_gate.shape[1]
    w_gu = jnp.concatenate([w_gate.reshape(d, d_ff // tf, tf), w_up.reshape(d, d_ff // tf, tf)],
                           axis=2).reshape(d, 2 * d_ff).astype(BF16)
    return wp, wgk, b_gk.reshape(1, -1), wq, wuk, wuv, wo_a, wo_b, w_gu, w_down.astype(BF16)


def _rope_tables(pos):
    half = MLA_ROPE // 2
    inv = ROPE_BASE ** (-jnp.arange(half, dtype=F32) / half)
    ang = pos.astype(F32)[:, None] * inv[None, :]
    cos, sin = jnp.cos(ang), jnp.sin(ang)
    return jnp.tile(cos, (1, LANES // half)), jnp.tile(jnp.concatenate([-sin, sin], axis=1), (1, LANES // MLA_ROPE))


def _forward(x_prompt, x_sample, cache_ckv, cache_kpe, state_gla, page_table, w_in, w_gk, b_gk, gla_norm_g,
             q_norm_g, w_uq, kv_norm_g, w_uk, w_uv, w_o, ln1_g, ln1_b, w_ffn_gate, w_ffn_up, w_ffn_down,
             ln2_g, ln2_b):
    batch, seq, d = x_prompt.shape
    dec_b, t_new, _ = x_sample.shape
    depth = w_in.shape[0]
    n_p, n_s = batch * seq, dec_b * t_new
    past = page_table.shape[1] * PAGE_SIZE
    alpha = (2.0 * depth) ** 0.25
    d_ff = w_ffn_gate.shape[2]
    tf = 256

    pos = jnp.concatenate([jnp.tile(jnp.arange(seq, dtype=jnp.int32), batch),
                           jnp.tile(past + jnp.arange(t_new, dtype=jnp.int32), dec_b)])
    cos, sin = _rope_tables(pos)

    x = jnp.concatenate([x_prompt.reshape(n_p, d), x_sample.reshape(n_s, d)], axis=0)
    xb = x.astype(BF16)
    outs = [[] for _ in range(6)]
    for l in range(depth):
        wp, wgk, bgk, wq, wuk, wuv, wo_a, wo_b, w_gu, w_dn = _prep_weights(
            w_in[l], w_gk[l], b_gk[l], w_uq[l], w_uk[l], w_uv[l], w_o[l],
            w_ffn_gate[l], w_ffn_up[l], w_ffn_down[l], tf)
        gn = gla_norm_g[l].reshape(1, -1)

        p = _matmul(xb, wp, F32, 1024, 512, f"proj{l}")
        og_p, s_p = _gla_prompt(p, wgk, bgk, gn, batch, seq, f"gla_prompt{l}")
        og_s, s_s = _gla_sample(p, wgk, bgk, gn, state_gla, l, n_p, dec_b, t_new, f"gla_sample{l}")
        q, ckv, kpe, kcat = _mla_prep(p, q_norm_g[l], kv_norm_g[l], wq, wuk, cos, sin, f"mla_prep{l}")

        om_p = _attn_prompt(q, kcat, wuv, batch, seq, f"attn_prompt{l}")
        q_s = q[:, n_p:, :].reshape(MLA_HEADS, dec_b, t_new, QK_PAD).transpose(1, 0, 2, 3)
        q_s = q_s.reshape(dec_b, MLA_HEADS * t_new, QK_PAD)
        ckv_s = ckv[n_p:].reshape(dec_b, t_new, KV_LORA)
        kpe_s = kpe[n_p:].reshape(dec_b, t_new, MLA_ROPE)
        ol_s = _attn_sample(page_table, q_s, cache_ckv, cache_kpe, ckv_s, kpe_s, l, f"attn_sample{l}")
        ol_s = ol_s.reshape(dec_b, MLA_HEADS, t_new, KV_LORA).transpose(1, 0, 2, 3)
        om_s = _head_out(ol_s.reshape(MLA_HEADS, n_s, KV_LORA), wuv, f"head_out{l}")

        o_gla = jnp.concatenate([og_p, og_s], axis=0)
        o_mla = jnp.concatenate([om_p, om_s], axis=0)
        h = _matmul2(o_gla, o_mla, wo_a, wo_b, 1024, 512, f"out_proj{l}")
        x1, x1b = _res_ln(x, h, ln1_g[l], ln1_b[l], alpha, 0, n_p + n_s, f"ln1_{l}")
        hh = _swiglu_up(x1b, w_gu, d_ff, tf, 1024, f"ffn_up{l}")
        f = _matmul(hh, w_dn, F32, 512, 256, f"ffn_down{l}")
        if l + 1 < depth:
            x, xb = _res_ln(x1, f, ln2_g[l], ln2_b[l], alpha, 0, n_p + n_s, f"ln2_{l}")
        else:
            y_p, _ = _res_ln(x1, f, ln2_g[l], ln2_b[l], alpha, 0, n_p, f"ln2_{l}p")
            y_s, _ = _res_ln(x1, f, ln2_g[l], ln2_b[l], alpha, n_p, n_s, f"ln2_{l}s")

        outs[0].append(ckv[:n_p].reshape(batch, seq, KV_LORA))
        outs[1].append(kpe[:n_p].reshape(batch, seq, MLA_ROPE))
        outs[2].append(s_p)
        outs[3].append(ckv_s)
        outs[4].append(kpe_s)
        outs[5].append(s_s)

    return (y_p.reshape(batch, seq, d), y_s.reshape(dec_b, t_new, d)) + tuple(jnp.stack(o) for o in outs)


def kernel(x_prompt, x_sample, cache_ckv, cache_kpe, state_gla, page_table, w_in, w_gk, b_gk, gla_norm_g, q_norm_g, w_uq, kv_norm_g, w_uk, w_uv, w_o, ln1_g, ln1_b, w_ffn_gate, w_ffn_up, w_ffn_down, ln2_g, ln2_b):
    return _forward(x_prompt, x_sample, cache_ckv, cache_kpe, state_gla, page_table, w_in, w_gk, b_gk,
                    gla_norm_g, q_norm_g, w_uq, kv_norm_g, w_uk, w_uv, w_o, ln1_g, ln1_b, w_ffn_gate,
                    w_ffn_up, w_ffn_down, ln2_g, ln2_b)
```

```python
import functools

import jax
import jax.numpy as jnp
from jax import lax
from jax.experimental import pallas as pl
from jax.experimental.pallas import tpu as pltpu

GLA_HEADS = 8
GLA_DK = 128
GLA_DV = 256
GLA_GATE_RANK = 16
GLA_GATE_NORM = 16.0
MLA_HEADS = 16
MLA_NOPE = 128
MLA_ROPE = 64
MLA_V = 128
Q_LORA = 896
KV_LORA = 512
MLA_SCALE = (MLA_NOPE + MLA_ROPE) ** -0.5
ROPE_BASE = 10000.0
PAGE_SIZE = 128
NEG = -1e30
LN_EPS = 1e-5
RMS_EPS = 1e-6

LANES = 128
GLA_SUB = 32
GLA_TILE = 128
QK_PAD = KV_LORA + LANES
PAGES_PER_STEP = 32
SEQS_PER_STEP = 1
ATTN_HEAD_GROUP = 16
ATTN_TK = 512
VMEM_LIMIT = 56 * 1024 * 1024

P_CQ = 0
P_CKV = 896
P_KPE = 1408
P_GLOW = 1536
P_HEAD = 2048
P_GV = 2048
P_GR = 4096
P_GQ = 6144
P_GK = 7168
P_COLS = 8192

BF16 = jnp.bfloat16
F32 = jnp.float32


def _cparams(n_axes):
    return pltpu.CompilerParams(dimension_semantics=("arbitrary",) * n_axes,
                                vmem_limit_bytes=VMEM_LIMIT)


def _pick(n, pref):
    if n <= pref:
        return n
    t = pref
    while n % t:
        t //= 2
    return t


def _bdot(a, b):
    return jnp.dot(a.astype(BF16), b.astype(BF16), preferred_element_type=F32)


def _bdot_nt(a, b):
    return lax.dot_general(a.astype(BF16), b.astype(BF16), (((1,), (1,)), ((), ())),
                           preferred_element_type=F32)


def _mm_kernel(x_ref, w_ref, o_ref):
    o_ref[...] = jnp.dot(x_ref[...], w_ref[...], preferred_element_type=F32).astype(o_ref.dtype)


def _matmul(x, w, layer, out_dtype, tm, tn, name):
    m, k = x.shape
    n = w.shape[2]
    tm, tn = _pick(m, tm), _pick(n, tn)
    return pl.pallas_call(
        _mm_kernel,
        grid=(m // tm, n // tn),
        in_specs=[pl.BlockSpec((tm, k), lambda i, j: (i, 0)),
                  pl.BlockSpec((None, k, tn), lambda i, j: (layer, 0, j))],
        out_specs=pl.BlockSpec((tm, tn), lambda i, j: (i, j)),
        out_shape=jax.ShapeDtypeStruct((m, n), out_dtype),
        compiler_params=_cparams(2),
        name=name,
    )(x, w)


def _mm_nt_kernel(x_ref, w_ref, o_ref):
    o_ref[...] = lax.dot_general(x_ref[...], w_ref[...], (((1,), (1,)), ((), ())),
                                 preferred_element_type=F32).astype(o_ref.dtype)


def _matmul_nt(x, w_t, layer, out_dtype, tm, tn, name):
    m, k = x.shape
    n = w_t.shape[1]
    tm, tn = _pick(m, tm), _pick(n, tn)
    return pl.pallas_call(
        _mm_nt_kernel,
        grid=(m // tm, n // tn),
        in_specs=[pl.BlockSpec((tm, k), lambda i, j: (i, 0)),
                  pl.BlockSpec((None, tn, k), lambda i, j: (layer, j, 0))],
        out_specs=pl.BlockSpec((tm, tn), lambda i, j: (i, j)),
        out_shape=jax.ShapeDtypeStruct((m, n), out_dtype),
        compiler_params=_cparams(2),
        name=name,
    )(x, w_t)


def _mm2_kernel(a_ref, b_ref, wa_ref, wb_ref, o_ref):
    acc = jnp.dot(a_ref[...], wa_ref[...], preferred_element_type=F32)
    acc = acc + jnp.dot(b_ref[...], wb_ref[...], preferred_element_type=F32)
    o_ref[...] = acc


def _matmul2(a, b, w, layer, tm, tn, name):
    m, ka = a.shape
    assert b.shape[1] == ka and w.shape[1] == 2 * ka
    n = w.shape[2]
    tm, tn = _pick(m, tm), _pick(n, tn)
    return pl.pallas_call(
        _mm2_kernel,
        grid=(m // tm, n // tn),
        in_specs=[pl.BlockSpec((tm, ka), lambda i, j: (i, 0)),
                  pl.BlockSpec((tm, ka), lambda i, j: (i, 0)),
                  pl.BlockSpec((None, ka, tn), lambda i, j: (layer, 0, j)),
                  pl.BlockSpec((None, ka, tn), lambda i, j: (layer, 1, j))],
        out_specs=pl.BlockSpec((tm, tn), lambda i, j: (i, j)),
        out_shape=jax.ShapeDtypeStruct((m, n), F32),
        compiler_params=_cparams(2),
        name=name,
    )(a, b, w, w)


def _swiglu_kernel(x_ref, wg_ref, wu_ref, o_ref):
    x = x_ref[...]
    g = jnp.dot(x, wg_ref[...], preferred_element_type=F32)
    u = jnp.dot(x, wu_ref[...], preferred_element_type=F32)
    o_ref[...] = (g * (1.0 / (1.0 + jnp.exp(-g))) * u).astype(o_ref.dtype)


def _swiglu_up(x, w_gate, w_up, layer, tf, tm, name):
    m, k = x.shape
    d_ff = w_gate.shape[2]
    tm = _pick(m, tm)
    w_spec = pl.BlockSpec((None, k, tf), lambda i, j: (layer, 0, j))
    return pl.pallas_call(
        _swiglu_kernel,
        grid=(m // tm, d_ff // tf),
        in_specs=[pl.BlockSpec((tm, k), lambda i, j: (i, 0)), w_spec, w_spec],
        out_specs=pl.BlockSpec((tm, tf), lambda i, j: (i, j)),
        out_shape=jax.ShapeDtypeStruct((m, d_ff), BF16),
        compiler_params=_cparams(2),
        name=name,
    )(x, w_gate, w_up)


def _ln_kernel(x_ref, h_ref, g_ref, b_ref, o_ref, ob_ref, *, alpha):
    y = alpha * x_ref[...] + h_ref[...]
    mu = jnp.mean(y, axis=-1, keepdims=True)
    d = y - mu
    var = jnp.mean(d * d, axis=-1, keepdims=True)
    o = d * lax.rsqrt(var + LN_EPS) * g_ref[...] + b_ref[...]
    o_ref[...] = o
    ob_ref[...] = o.astype(BF16)


def _res_ln(x, h, g, b, layer, alpha, row0, rows, name):
    d = x.shape[1]
    tm = _pick(rows, 256)
    assert row0 % tm == 0
    off = row0 // tm
    return pl.pallas_call(
        functools.partial(_ln_kernel, alpha=alpha),
        grid=(rows // tm,),
        in_specs=[pl.BlockSpec((tm, d), lambda i: (i + off, 0)),
                  pl.BlockSpec((tm, d), lambda i: (i + off, 0)),
                  pl.BlockSpec((None, 1, d), lambda i: (layer, 0, 0)),
                  pl.BlockSpec((None, 1, d), lambda i: (layer, 0, 0))],
        out_specs=[pl.BlockSpec((tm, d), lambda i: (i, 0)),
                   pl.BlockSpec((tm, d), lambda i: (i, 0))],
        out_shape=[jax.ShapeDtypeStruct((rows, d), F32),
                   jax.ShapeDtypeStruct((rows, d), BF16)],
        compiler_params=_cparams(1),
        name=name,
    )(x, h, g, b)


def _log_sigmoid(z):
    return jnp.minimum(z, 0.0) - jnp.log(1.0 + jnp.exp(-jnp.abs(z)))


def _split3(x):
    hi = x.astype(BF16)
    r1 = x - hi.astype(F32)
    mid = r1.astype(BF16)
    lo = (r1 - mid.astype(F32)).astype(BF16)
    return jnp.concatenate([hi, mid, lo], axis=1)


def _sum3(y, w):
    return y[:, :w] + y[:, w:2 * w] + y[:, 2 * w:3 * w]


def _iota2(shape):
    return lax.broadcasted_iota(jnp.int32, shape, 0), lax.broadcasted_iota(jnp.int32, shape, 1)


def _idiv(x, n):
    return x >> (n.bit_length() - 1) if n & (n - 1) == 0 else x // n


def _imod(x, n):
    return x & (n - 1) if n & (n - 1) == 0 else x % n


def _ones_where(mask):
    return jnp.where(mask, 1.0, 0.0).astype(BF16)


def _log_gate(gl, wgk, bgk):
    z = jnp.dot(gl.astype(BF16), wgk, preferred_element_type=F32) + bgk
    return _log_sigmoid(z) / GLA_GATE_NORM


def _gla_finish(o, gn, gr):
    ms = jnp.mean(o * o, axis=-1, keepdims=True)
    on = o * lax.rsqrt(ms + RMS_EPS) * gn
    return on * (gr * (1.0 / (1.0 + jnp.exp(-gr))))


def _rows_bcast(x, row, n):
    return jnp.broadcast_to(x[row:row + 1, :], (n, x.shape[1]))


def _gla_prompt_kernel(q_ref, k_ref, v_ref, gl_ref, gr_ref, wgk_ref, bgk_ref, gn_ref,
                       o_ref, sfin_ref, st_ref, *, n_tiles):
    t_blk = pl.program_id(2)

    @pl.when(t_blk == 0)
    def _():
        st_ref[...] = jnp.zeros_like(st_ref)

    n = GLA_TILE
    n_sub = n // GLA_SUB
    r, c = _iota2((n, n))
    tri = _ones_where(c <= r)
    diag_mask = (_idiv(r, GLA_SUB) == _idiv(c, GLA_SUB)) & (c <= r)
    c_sub = lax.broadcasted_iota(jnp.int32, (GLA_SUB, n), 1)
    scale = GLA_DK ** -0.5

    for t in range(n_tiles):
        rows = pl.ds(t * n, n)
        qs = q_ref[rows, :] * scale
        k = k_ref[rows, :]
        v = v_ref[rows, :]
        la = _log_gate(gl_ref[rows, :], wgk_ref[...], bgk_ref[...])
        g = _sum3(jnp.dot(tri, _split3(la), preferred_element_type=F32), GLA_DK)
        g_last = g[n - 1:n, :]

        g_mid = jnp.concatenate(
            [_rows_bcast(g, s * GLA_SUB + GLA_SUB // 2 - 1, GLA_SUB) for s in range(n_sub)], axis=0)
        g_start = jnp.concatenate(
            [jnp.zeros((GLA_SUB, GLA_DK), F32)]
            + [_rows_bcast(g, s * GLA_SUB - 1, GLA_SUB) for s in range(1, n_sub)], axis=0)

        att = jnp.where(diag_mask, _bdot_nt(qs * jnp.exp(g - g_mid), k * jnp.exp(g_mid - g)), 0.0)
        q_off = qs * jnp.exp(g - g_start)
        blocks = [att[:GLA_SUB, :]]
        for s in range(1, n_sub):
            k_off = k * jnp.exp(jnp.minimum(g[s * GLA_SUB - 1:s * GLA_SUB, :] - g, 0.0))
            a = _bdot_nt(q_off[s * GLA_SUB:(s + 1) * GLA_SUB, :], k_off)
            blocks.append(jnp.where(c_sub < s * GLA_SUB, a, att[s * GLA_SUB:(s + 1) * GLA_SUB, :]))
        att = jnp.concatenate(blocks, axis=0)

        st = st_ref[...]
        o = _bdot(att, v) + _bdot_nt(qs * jnp.exp(g), st)
        k_tail = k * jnp.exp(g_last - g)
        st_ref[...] = st * jnp.exp(g_last) + _bdot(v.T, k_tail)
        o_ref[rows, :] = _gla_finish(o, gn_ref[...], gr_ref[rows, :]).astype(o_ref.dtype)

    @pl.when(t_blk == pl.num_programs(2) - 1)
    def _():
        sfin_ref[...] = st_ref[...].T


def _gla_prompt(p, wgk, bgk, gn, layer, batch, seq, name):
    ct = _pick(seq, 512)
    n_t = seq // ct
    w128 = lambda base, h: base // GLA_DK + h
    w256 = lambda base, h: base // GLA_DV + h
    row = lambda b, h, t: b * n_t + t
    return pl.pallas_call(
        functools.partial(_gla_prompt_kernel, n_tiles=ct // GLA_TILE),
        grid=(batch, GLA_HEADS, n_t),
        in_specs=[pl.BlockSpec((ct, GLA_DK), lambda b, h, t: (row(b, h, t), w128(P_GQ, h))),
                  pl.BlockSpec((ct, GLA_DK), lambda b, h, t: (row(b, h, t), w128(P_GK, h))),
                  pl.BlockSpec((ct, GLA_DV), lambda b, h, t: (row(b, h, t), w256(P_GV, h))),
                  pl.BlockSpec((ct, LANES), lambda b, h, t: (row(b, h, t), P_GLOW // LANES)),
                  pl.BlockSpec((ct, GLA_DV), lambda b, h, t: (row(b, h, t), w256(P_GR, h))),
                  pl.BlockSpec((None, LANES, GLA_DK), lambda b, h, t: (layer, 0, h)),
                  pl.BlockSpec((None, 1, GLA_DK), lambda b, h, t: (layer, 0, h)),
                  pl.BlockSpec((None, 1, GLA_DV), lambda b, h, t: (layer, 0, 0))],
        out_specs=[pl.BlockSpec((ct, GLA_DV), lambda b, h, t: (row(b, h, t), h)),
                   pl.BlockSpec((None, None, GLA_DK, GLA_DV), lambda b, h, t: (b, h, 0, 0))],
        out_shape=[jax.ShapeDtypeStruct((batch * seq, GLA_HEADS * GLA_DV), BF16),
                   jax.ShapeDtypeStruct((batch, GLA_HEADS, GLA_DK, GLA_DV), F32)],
        scratch_shapes=[pltpu.VMEM((GLA_DV, GLA_DK), F32)],
        compiler_params=_cparams(3),
        name=name,
    )(p, p, p, p, p, wgk, bgk, gn)


def _gla_sample_kernel(q_ref, k_ref, v_ref, gl_ref, gr_ref, wgk_ref, bgk_ref, gn_ref, s0_ref,
                       o_ref, sfin_ref, oi_ref, bl_ref, kt_ref, vt_ref, qd_ref, *, t_new, n_b):
    n = n_b * t_new
    r, c = _iota2((n, n))
    same = _idiv(r, t_new) == _idiv(c, t_new)
    causal = same & (c <= r)
    tri = _ones_where(causal)
    ones = _ones_where(same)
    scale = GLA_DK ** -0.5

    qs = q_ref[...] * scale
    k = k_ref[...]
    v = v_ref[...]
    la = _log_gate(gl_ref[...], wgk_ref[...], bgk_ref[...])
    la3 = _split3(la)
    b = _sum3(jnp.dot(tri, la3, preferred_element_type=F32), GLA_DK)
    b_last = _sum3(jnp.dot(ones, la3, preferred_element_type=F32), GLA_DK)
    q_dec = qs * jnp.exp(b)
    att = jnp.where(causal, _bdot_nt(q_dec, k * jnp.exp(-b)), 0.0)
    o_intra = _bdot(att, v)

    bl_ref[...] = b_last
    kt_ref[...] = (k * jnp.exp(b_last - b)).astype(BF16)
    vt_ref[...] = v.T.astype(BF16)
    qd_ref[...] = q_dec.astype(BF16)
    oi_ref[...] = jnp.zeros_like(oi_ref)
    row_seq = _idiv(lax.broadcasted_iota(jnp.int32, (n, 1), 0), t_new)

    def body(i, carry):
        st = s0_ref[i].T
        mine = row_seq == i
        o_full = lax.dot_general(qd_ref[...], st.astype(BF16), (((1,), (1,)), ((), ())),
                                 preferred_element_type=F32)
        oi_ref[...] += jnp.where(mine, o_full, 0.0)
        decay = jnp.exp(bl_ref[pl.ds(pl.multiple_of(i * t_new, t_new), t_new), :][0:1, :])
        kt = jnp.where(mine, kt_ref[...], jnp.zeros_like(kt_ref))
        st_new = st * decay + jnp.dot(vt_ref[...], kt, preferred_element_type=F32)
        sfin_ref[i] = st_new.T
        return carry

    lax.fori_loop(0, n_b, body, 0)
    o_ref[...] = _gla_finish(oi_ref[...] + o_intra, gn_ref[...], gr_ref[...]).astype(o_ref.dtype)


def _gla_sample(p, wgk, bgk, gn, s0, layer, row0, dec_b, t_new, name):
    n_b = GLA_TILE // t_new
    n = n_b * t_new
    r0 = row0 // n
    w128 = lambda base, h: base // GLA_DK + h
    w256 = lambda base, h: base // GLA_DV + h
    return pl.pallas_call(
        functools.partial(_gla_sample_kernel, t_new=t_new, n_b=n_b),
        grid=(dec_b // n_b, GLA_HEADS),
        in_specs=[pl.BlockSpec((n, GLA_DK), lambda i, h: (r0 + i, w128(P_GQ, h))),
                  pl.BlockSpec((n, GLA_DK), lambda i, h: (r0 + i, w128(P_GK, h))),
                  pl.BlockSpec((n, GLA_DV), lambda i, h: (r0 + i, w256(P_GV, h))),
                  pl.BlockSpec((n, LANES), lambda i, h: (r0 + i, P_GLOW // LANES)),
                  pl.BlockSpec((n, GLA_DV), lambda i, h: (r0 + i, w256(P_GR, h))),
                  pl.BlockSpec((None, LANES, GLA_DK), lambda i, h: (layer, 0, h)),
                  pl.BlockSpec((None, 1, GLA_DK), lambda i, h: (layer, 0, h)),
                  pl.BlockSpec((None, 1, GLA_DV), lambda i, h: (layer, 0, 0)),
                  pl.BlockSpec((None, n_b, None, GLA_DK, GLA_DV), lambda i, h: (layer, i, h, 0, 0))],
        out_specs=[pl.BlockSpec((n, GLA_DV), lambda i, h: (i, h)),
                   pl.BlockSpec((n_b, None, GLA_DK, GLA_DV), lambda i, h: (i, h, 0, 0))],
        out_shape=[jax.ShapeDtypeStruct((dec_b * t_new, GLA_HEADS * GLA_DV), BF16),
                   jax.ShapeDtypeStruct(s0.shape[1:], F32)],
        scratch_shapes=[pltpu.VMEM((n, GLA_DV), F32),
                        pltpu.VMEM((n, GLA_DK), F32),
                        pltpu.VMEM((n, GLA_DK), BF16),
                        pltpu.VMEM((GLA_DV, n), BF16),
                        pltpu.VMEM((n, GLA_DK), BF16)],
        compiler_params=_cparams(2),
        name=name,
    )(p, p, p, p, p, wgk, bgk, gn, s0)


def _rms(x, g):
    return x * lax.rsqrt(jnp.mean(x * x, axis=-1, keepdims=True) + RMS_EPS) * g


def _mla_prep_kernel(p_ref, qg_ref, kg_ref, wq_ref, wuk_ref, cos_ref, sin_ref,
                     qt_ref, ckv_ref, kpe_ref, kcat_ref, vt_ref):
    tm = p_ref.shape[0]
    nope_w = MLA_HEADS * MLA_NOPE
    rope_w = MLA_HEADS * MLA_ROPE
    cq = p_ref[:, P_CQ:P_CQ + Q_LORA]
    qa = jnp.dot(_rms(cq, qg_ref[...]).astype(BF16), wq_ref[...], preferred_element_type=F32)
    cos = cos_ref[...]
    sin = sin_ref[...]
    low = lax.broadcasted_iota(jnp.int32, cos.shape, 1) < MLA_ROPE
    for j in range(MLA_HEADS // 2):
        x = qa[:, nope_w + j * LANES:nope_w + (j + 1) * LANES]
        xs = qa[:, nope_w + rope_w + j * LANES:nope_w + rope_w + (j + 1) * LANES]
        rot = x * cos + xs * sin
        tails = (jnp.where(low, rot, 0.0), jnp.where(low, pltpu.roll(rot, MLA_ROPE, 1), 0.0))
        for u in range(2):
            h = 2 * j + u
            q_lat = jnp.dot(qa[:, h * MLA_NOPE:(h + 1) * MLA_NOPE].astype(BF16), wuk_ref[h],
                            preferred_element_type=F32)
            q_t = jnp.concatenate([q_lat, tails[u]], axis=1).T.astype(BF16)
            for tb in range(tm // PAGE_SIZE):
                qt_ref[tb, :, h * PAGE_SIZE:(h + 1) * PAGE_SIZE] = q_t[:, tb * PAGE_SIZE:(tb + 1) * PAGE_SIZE]

    ckv = _rms(p_ref[:, P_CKV:P_CKV + KV_LORA], kg_ref[...])
    kk = p_ref[:, P_KPE:P_KPE + LANES] * jnp.where(low, cos, sin)
    kpe = kk + pltpu.roll(kk, MLA_ROPE, 1)
    ckv_ref[...] = ckv
    kpe_ref[...] = kpe[:, :MLA_ROPE]
    kcat_ref[:, :KV_LORA] = ckv.astype(BF16)
    kcat_ref[:, KV_LORA:] = jnp.where(low, kpe, 0.0).astype(BF16)
    vt_ref[...] = ckv.T.astype(BF16)


def _mla_prep(p, qg, kg, wq, wuk, layer, cos, sin, name):
    t = p.shape[0]
    tm = _pick(t, 256)
    assert tm % PAGE_SIZE == 0
    nb = tm // PAGE_SIZE
    return pl.pallas_call(
        _mla_prep_kernel,
        grid=(t // tm,),
        in_specs=[pl.BlockSpec((tm, P_HEAD), lambda i: (i, 0)),
                  pl.BlockSpec((None, 1, Q_LORA), lambda i: (layer, 0, 0)),
                  pl.BlockSpec((None, 1, KV_LORA), lambda i: (layer, 0, 0)),
                  pl.BlockSpec((None,) + wq.shape[1:], lambda i: (layer, 0, 0)),
                  pl.BlockSpec((None,) + wuk.shape[1:], lambda i: (layer, 0, 0, 0)),
                  pl.BlockSpec((tm, LANES), lambda i: (i, 0)),
                  pl.BlockSpec((tm, LANES), lambda i: (i, 0))],
        out_specs=[pl.BlockSpec((nb, QK_PAD, MLA_HEADS * PAGE_SIZE), lambda i: (i, 0, 0)),
                   pl.BlockSpec((tm, KV_LORA), lambda i: (i, 0)),
                   pl.BlockSpec((tm, MLA_ROPE), lambda i: (i, 0)),
                   pl.BlockSpec((tm, QK_PAD), lambda i: (i, 0)),
                   pl.BlockSpec((KV_LORA, tm), lambda i: (0, i))],
        out_shape=[jax.ShapeDtypeStruct((t // PAGE_SIZE, QK_PAD, MLA_HEADS * PAGE_SIZE), BF16),
                   jax.ShapeDtypeStruct((t, KV_LORA), F32),
                   jax.ShapeDtypeStruct((t, MLA_ROPE), F32),
                   jax.ShapeDtypeStruct((t, QK_PAD), BF16),
                   jax.ShapeDtypeStruct((KV_LORA, t), BF16)],
        compiler_params=_cparams(1),
        name=name,
    )(p, qg, kg, wq, wuk, cos, sin)


def _attn_prompt_kernel(qt_ref, k_ref, vt_ref, wuv_ref, o_ref, m_ref, l_ref, acc_ref, *, tq, tk):
    qi = pl.program_id(1)
    ki = pl.program_id(2)
    last_k = (qi * tq + tq - 1) // tk
    cols = MLA_HEADS * tq

    @pl.when(ki == 0)
    def _():
        m_ref[...] = jnp.full_like(m_ref, NEG)
        l_ref[...] = jnp.zeros_like(l_ref)
        acc_ref[...] = jnp.zeros_like(acc_ref)

    def step(masked):
        k = k_ref[...]
        vt = vt_ref[...]
        cw = ATTN_HEAD_GROUP * tq
        if masked:
            key, col = _iota2((tk, cw))
            visible = ki * tk + key <= qi * tq + _imod(col, tq)
        for j in range(cols // cw):
            cs = slice(j * cw, (j + 1) * cw)
            s_t = jnp.dot(k, qt_ref[:, cs], preferred_element_type=F32) * MLA_SCALE
            if masked:
                s_t = jnp.where(visible, s_t, NEG)
            m_old = m_ref[:, cs]
            m_new = jnp.maximum(m_old, jnp.max(s_t, axis=0, keepdims=True))
            corr = jnp.exp(m_old - m_new)
            p_t = jnp.exp(s_t - m_new)
            l_ref[:, cs] = l_ref[:, cs] * corr + jnp.sum(p_t, axis=0, keepdims=True)
            acc_ref[:, cs] = acc_ref[:, cs] * corr + jnp.dot(vt, p_t.astype(BF16),
                                                             preferred_element_type=F32)
            m_ref[:, cs] = m_new

    @pl.when((ki * tk + tk - 1 <= qi * tq) & (ki <= last_k))
    def _():
        step(False)

    @pl.when((ki * tk + tk - 1 > qi * tq) & (ki <= last_k))
    def _():
        step(True)

    @pl.when(ki == last_k)
    def _():
        for h in range(MLA_HEADS):
            cs = slice(h * tq, (h + 1) * tq)
            o_t = (acc_ref[:, cs] / l_ref[:, cs]).astype(BF16)
            o_ref[:, h * MLA_V:(h + 1) * MLA_V] = lax.dot_general(
                o_t, wuv_ref[h], (((0,), (0,)), ((), ())), preferred_element_type=F32).astype(o_ref.dtype)


def _attn_prompt(qt, kcat, vt, wuv, layer, batch, seq, name):
    tq = PAGE_SIZE
    tk = _pick(seq, ATTN_TK)
    nq, nk = seq // tq, seq // tk

    def k_blk(b, qi, ki):
        return b * nk + jnp.minimum(ki, (qi * tq + tq - 1) // tk)

    return pl.pallas_call(
        functools.partial(_attn_prompt_kernel, tq=tq, tk=tk),
        grid=(batch, nq, nk),
        in_specs=[pl.BlockSpec((None, QK_PAD, MLA_HEADS * tq), lambda b, qi, ki: (b * nq + qi, 0, 0)),
                  pl.BlockSpec((tk, QK_PAD), lambda b, qi, ki: (k_blk(b, qi, ki), 0)),
                  pl.BlockSpec((KV_LORA, tk), lambda b, qi, ki: (0, k_blk(b, qi, ki))),
                  pl.BlockSpec((None,) + wuv.shape[1:], lambda b, qi, ki: (layer, 0, 0, 0))],
        out_specs=pl.BlockSpec((tq, MLA_HEADS * MLA_V), lambda b, qi, ki: (b * nq + qi, 0)),
        out_shape=jax.ShapeDtypeStruct((batch * seq, MLA_HEADS * MLA_V), BF16),
        scratch_shapes=[pltpu.VMEM((1, MLA_HEADS * tq), F32),
                        pltpu.VMEM((1, MLA_HEADS * tq), F32),
                        pltpu.VMEM((KV_LORA, MLA_HEADS * tq), F32)],
        compiler_params=_cparams(3),
        name=name,
    )(qt, kcat, vt, wuv)


def _col(row_vec, n):
    return jnp.broadcast_to(row_vec, (n, n)).T[:, 0:1]


def _attn_sample_kernel(pt_ref, qt_ref, *refs, t_new, n_pg, n_b):
    n_in = n_b * n_pg
    ck_refs = refs[:n_in]
    kp_refs = refs[n_in:2 * n_in]
    cn_ref, kn_ref, o_ref, m_ref, l_ref, acc_ref = refs[2 * n_in:]
    g = pl.program_id(1)
    rows = MLA_HEADS * t_new
    tn = (((0,), (0,)), ((), ()))

    @pl.when(g == 0)
    def _():
        m_ref[...] = jnp.full_like(m_ref, NEG)
        l_ref[...] = jnp.zeros_like(l_ref)
        acc_ref[...] = jnp.zeros_like(acc_ref)

    def update(u, s_t, kv):
        m_old = m_ref[u]
        m_new = jnp.maximum(m_old, jnp.max(s_t, axis=0, keepdims=True))
        corr = jnp.exp(m_old - m_new)
        p_t = jnp.exp(s_t - m_new)
        l_ref[u] = l_ref[u] * corr + jnp.sum(p_t, axis=0, keepdims=True)
        m_ref[u] = m_new
        acc_ref[u] = acc_ref[u] * _col(corr, rows) + jnp.dot(p_t.T.astype(BF16), kv,
                                                             preferred_element_type=F32)

    for u in range(n_b):
        q_lat = qt_ref[u, :KV_LORA, :]
        q_pe = qt_ref[u, KV_LORA:KV_LORA + MLA_ROPE, :]
        kv = jnp.concatenate([ck_refs[u * n_pg + i][...].astype(BF16) for i in range(n_pg)], axis=0)
        s_pe = jnp.concatenate(
            [lax.dot_general(kp_refs[u * n_pg + i][...].astype(BF16), q_pe, tn, preferred_element_type=F32)
             for i in range(n_pg)], axis=0)
        s_t = (jnp.dot(kv, q_lat, preferred_element_type=F32) + s_pe) * MLA_SCALE
        update(u, s_t, kv)

    @pl.when(g == pl.num_programs(1) - 1)
    def _():
        zc = jnp.zeros((PAGE_SIZE - t_new, KV_LORA), F32)
        zp = jnp.zeros((PAGE_SIZE - t_new, MLA_ROPE), F32)
        key, qrow = _iota2((PAGE_SIZE, rows))
        causal = key <= _imod(qrow, t_new)
        for u in range(n_b):
            kv = jnp.concatenate([cn_ref[u], zc], axis=0).astype(BF16)
            kp = jnp.concatenate([kn_ref[u], zp], axis=0).astype(BF16)
            s_t = (jnp.dot(kv, qt_ref[u, :KV_LORA, :], preferred_element_type=F32)
                   + jnp.dot(kp, qt_ref[u, KV_LORA:KV_LORA + MLA_ROPE, :], preferred_element_type=F32))
            update(u, jnp.where(causal, s_t * MLA_SCALE, NEG), kv)
            o_ref[u] = (acc_ref[u] / _col(l_ref[u], rows)).astype(o_ref.dtype)


def _attn_sample(page_table, q_t, cache_ckv, cache_kpe_t, ckv_new, kpe_new, layer, name):
    dec_b, _, rows = q_t.shape
    t_new = rows // MLA_HEADS
    n_pages = page_table.shape[1]
    n_pg = _pick(n_pages, PAGES_PER_STEP)
    n_b = _pick(dec_b, SEQS_PER_STEP)

    def page(u, i):
        return lambda b, g, pt: (layer, pt[b * n_b + u, g * n_pg + i], 0, 0)

    slots = [(u, i) for u in range(n_b) for i in range(n_pg)]
    in_specs = [pl.BlockSpec((n_b, QK_PAD, rows), lambda b, g, pt: (b, 0, 0))]
    in_specs += [pl.BlockSpec((None, None, PAGE_SIZE, KV_LORA), page(u, i)) for u, i in slots]
    in_specs += [pl.BlockSpec((None, None, MLA_ROPE, PAGE_SIZE), page(u, i)) for u, i in slots]
    in_specs += [pl.BlockSpec((n_b, t_new, KV_LORA), lambda b, g, pt: (b, 0, 0)),
                 pl.BlockSpec((n_b, t_new, MLA_ROPE), lambda b, g, pt: (b, 0, 0))]
    return pl.pallas_call(
        functools.partial(_attn_sample_kernel, t_new=t_new, n_pg=n_pg, n_b=n_b),
        grid_spec=pltpu.PrefetchScalarGridSpec(
            num_scalar_prefetch=1,
            grid=(dec_b // n_b, n_pages // n_pg),
            in_specs=in_specs,
            out_specs=pl.BlockSpec((n_b, rows, KV_LORA), lambda b, g, pt: (b, 0, 0)),
            scratch_shapes=[pltpu.VMEM((n_b, 1, rows), F32),
                            pltpu.VMEM((n_b, 1, rows), F32),
                            pltpu.VMEM((n_b, rows, KV_LORA), F32)]),
        out_shape=jax.ShapeDtypeStruct((dec_b, rows, KV_LORA), BF16),
        compiler_params=_cparams(2),
        name=name,
    )(page_table, q_t, *([cache_ckv] * len(slots)), *([cache_kpe_t] * len(slots)), ckv_new, kpe_new)


def _head_out_kernel(x_ref, w_ref, o_ref):
    o_ref[...] = jnp.dot(x_ref[...], w_ref[...], preferred_element_type=F32).astype(o_ref.dtype)


def _head_out(x, wuv, layer, name):
    h, t, c = x.shape
    return pl.pallas_call(
        _head_out_kernel,
        grid=(h,),
        in_specs=[pl.BlockSpec((None, t, c), lambda i: (i, 0, 0)),
                  pl.BlockSpec((None, None, c, MLA_V), lambda i: (layer, i, 0, 0))],
        out_specs=pl.BlockSpec((t, MLA_V), lambda i: (0, i)),
        out_shape=jax.ShapeDtypeStruct((t, h * MLA_V), BF16),
        compiler_params=_cparams(1),
        name=name,
    )(x, wuv)


def _prep_weights(w_in, w_gk, w_uq, w_uk, w_uv):
    depth = w_in.shape[0]
    wt = jnp.swapaxes(w_in, 1, 2)
    s_gq, s_gk, s_gv, s_gl, s_gr, s_cq, s_ckv, s_kpe = jnp.split(
        wt, [1024, 2048, 4096, 4112, 6160, 7056, 7568], axis=1)
    half = MLA_ROPE // 2
    kpe_sw = jnp.concatenate([s_kpe[:, half:], s_kpe[:, :half]], axis=1)
    head = jnp.concatenate([s_cq, s_ckv, s_kpe, kpe_sw, s_gl], axis=1)
    head = jnp.pad(head, ((0, 0), (0, P_HEAD - head.shape[1]), (0, 0)))
    wp_t = jnp.concatenate([head, s_gv, s_gr, s_gq, s_gk], axis=1).astype(BF16)

    wgk = jnp.pad(w_gk, ((0, 0), (0, LANES - GLA_GATE_RANK), (0, 0))).astype(BF16)
    nope = w_uq[..., :MLA_NOPE].reshape(depth, Q_LORA, -1)
    pe = w_uq[..., MLA_NOPE:]
    pe_sw = jnp.concatenate([pe[..., half:], pe[..., :half]], axis=-1)
    wq = jnp.concatenate([nope, pe.reshape(depth, Q_LORA, -1), pe_sw.reshape(depth, Q_LORA, -1)],
                         axis=2).astype(BF16)
    wuk = jnp.transpose(w_uk, (0, 2, 3, 1)).astype(BF16)
    wuv = jnp.transpose(w_uv, (0, 2, 1, 3)).astype(BF16)
    return wp_t, wgk, wq, wuk, wuv


def _rope_tables(pos):
    half = MLA_ROPE // 2
    inv = ROPE_BASE ** (-jnp.arange(half, dtype=F32) / half)
    ang = pos.astype(F32)[:, None] * inv[None, :]
    cos, sin = jnp.cos(ang), jnp.sin(ang)
    return jnp.tile(cos, (1, LANES // half)), jnp.tile(jnp.concatenate([-sin, sin], axis=1), (1, LANES // MLA_ROPE))


def _forward(x_prompt, x_sample, cache_ckv, cache_kpe, state_gla, page_table, w_in, w_gk, b_gk, gla_norm_g,
             q_norm_g, w_uq, kv_norm_g, w_uk, w_uv, w_o, ln1_g, ln1_b, w_ffn_gate, w_ffn_up, w_ffn_down,
             ln2_g, ln2_b):
    batch, seq, d = x_prompt.shape
    dec_b, t_new, _ = x_sample.shape
    depth = w_in.shape[0]
    n_p, n_s = batch * seq, dec_b * t_new
    past = page_table.shape[1] * PAGE_SIZE
    alpha = (2.0 * depth) ** 0.25
    assert n_p % PAGE_SIZE == 0 and n_s % PAGE_SIZE == 0 and PAGE_SIZE % t_new == 0

    pos = jnp.concatenate([jnp.tile(jnp.arange(seq, dtype=jnp.int32), batch),
                           jnp.tile(past + jnp.arange(t_new, dtype=jnp.int32), dec_b)])
    cos, sin = _rope_tables(pos)
    cache_kpe_t = jnp.swapaxes(cache_kpe, 2, 3)

    wp_t, wgk, wq, wuk, wuv = _prep_weights(w_in, w_gk, w_uq, w_uk, w_uv)
    wo, wg, wu, wd = (w.astype(BF16) for w in (w_o, w_ffn_gate, w_ffn_up, w_ffn_down))
    row = lambda v: v.reshape(depth, 1, -1)
    bgk, gn, qg, kg = row(b_gk), row(gla_norm_g), row(q_norm_g), row(kv_norm_g)
    g1, b1, g2, b2 = row(ln1_g), row(ln1_b), row(ln2_g), row(ln2_b)

    x = jnp.concatenate([x_prompt.reshape(n_p, d), x_sample.reshape(n_s, d)], axis=0)
    xb = x.astype(BF16)
    outs = [[] for _ in range(6)]
    seqs_per_blk = PAGE_SIZE // t_new
    for l in range(depth):
        p = _matmul_nt(xb, wp_t, l, F32, 1024, 512, f"proj{l}")
        og_p, s_p = _gla_prompt(p, wgk, bgk, gn, l, batch, seq, f"gla_prompt{l}")
        og_s, s_s = _gla_sample(p, wgk, bgk, gn, state_gla, l, n_p, dec_b, t_new, f"gla_sample{l}")
        qt, ckv, kpe, kcat, vt = _mla_prep(p, qg, kg, wq, wuk, l, cos, sin, f"mla_prep{l}")

        om_p = _attn_prompt(qt, kcat, vt, wuv, l, batch, seq, f"attn_prompt{l}")
        q_s = qt[n_p // PAGE_SIZE:].reshape(-1, QK_PAD, MLA_HEADS, seqs_per_blk, t_new)
        q_s = q_s.transpose(0, 3, 1, 2, 4).reshape(dec_b, QK_PAD, MLA_HEADS * t_new)
        ckv_s = ckv[n_p:].reshape(dec_b, t_new, KV_LORA)
        kpe_s = kpe[n_p:].reshape(dec_b, t_new, MLA_ROPE)
        ol_s = _attn_sample(page_table, q_s, cache_ckv, cache_kpe_t, ckv_s, kpe_s, l, f"attn_sample{l}")
        ol_s = ol_s.reshape(dec_b, MLA_HEADS, t_new, KV_LORA).transpose(1, 0, 2, 3)
        om_s = _head_out(ol_s.reshape(MLA_HEADS, n_s, KV_LORA), wuv, l, f"head_out{l}")

        o_gla = jnp.concatenate([og_p, og_s], axis=0)
        o_mla = jnp.concatenate([om_p, om_s], axis=0)
        h = _matmul2(o_gla, o_mla, wo, l, 1024, 512, f"out_proj{l}")
        x1, x1b = _res_ln(x, h, g1, b1, l, alpha, 0, n_p + n_s, f"ln1_{l}")
        hh = _swiglu_up(x1b, wg, wu, l, 256, 1024, f"ffn_up{l}")
        f = _matmul(hh, wd, l, F32, 512, 256, f"ffn_down{l}")
        if l + 1 < depth:
            x, xb = _res_ln(x1, f, g2, b2, l, alpha, 0, n_p + n_s, f"ln2_{l}")
        else:
            y_p, _ = _res_ln(x1, f, g2, b2, l, alpha, 0, n_p, f"ln2_{l}p")
            y_s, _ = _res_ln(x1, f, g2, b2, l, alpha, n_p, n_s, f"ln2_{l}s")

        outs[0].append(ckv[:n_p].reshape(batch, seq, KV_LORA))
        outs[1].append(kpe[:n_p].reshape(batch, seq, MLA_ROPE))
        outs[2].append(s_p)
        outs[3].append(ckv_s)
        outs[4].append(kpe_s)
        outs[5].append(s_s)

    return (y_p.reshape(batch, seq, d), y_s.reshape(dec_b, t_new, d)) + tuple(jnp.stack(o) for o in outs)


def kernel(x_prompt, x_sample, cache_ckv, cache_kpe, state_gla, page_table, w_in, w_gk, b_gk, gla_norm_g, q_norm_g, w_uq, kv_norm_g, w_uk, w_uv, w_o, ln1_g, ln1_b, w_ffn_gate, w_ffn_up, w_ffn_down, ln2_g, ln2_b):
    return _forward(x_prompt, x_sample, cache_ckv, cache_kpe, state_gla, page_table, w_in, w_gk, b_gk,
                    gla_norm_g, q_norm_g, w_uq, kv_norm_g, w_uk, w_uv, w_o, ln1_g, ln1_b, w_ffn_gate,
                    w_ffn_up, w_ffn_down, ln2_g, ln2_b)
```

```python
import functools

import jax
import jax.numpy as jnp
from jax import lax
from jax.experimental import pallas as pl
from jax.experimental.pallas import tpu as pltpu

GLA_HEADS = 8
GLA_DK = 128
GLA_DV = 256
GLA_GATE_RANK = 16
GLA_GATE_NORM = 16.0
MLA_HEADS = 16
MLA_NOPE = 128
MLA_ROPE = 64
MLA_V = 128
Q_LORA = 896
KV_LORA = 512
MLA_SCALE = (MLA_NOPE + MLA_ROPE) ** -0.5
ROPE_BASE = 10000.0
PAGE_SIZE = 128
NEG = -1e30
LN_EPS = 1e-5
RMS_EPS = 1e-6

LANES = 128
GLA_SUB = 32
GLA_TILE = 128
QK_PAD = KV_LORA + LANES
PAGES_PER_STEP = 32
SEQS_PER_STEP = 1
ATTN_HEAD_GROUP = 16
ATTN_TK = 512
VMEM_LIMIT = 56 * 1024 * 1024

P_CQ = 0
P_CKV = 896
P_KPE = 1408
P_GLOW = 1536
P_HEAD = 2048
P_GV = 2048
P_GR = 4096
P_GQ = 6144
P_GK = 7168
P_COLS = 8192

BF16 = jnp.bfloat16
F32 = jnp.float32


def _cparams(n_axes):
    return pltpu.CompilerParams(dimension_semantics=("arbitrary",) * n_axes,
                                vmem_limit_bytes=VMEM_LIMIT)


def _pick(n, pref):
    if n <= pref:
        return n
    t = pref
    while n % t:
        t //= 2
    return t


def _bdot(a, b):
    return jnp.dot(a.astype(BF16), b.astype(BF16), preferred_element_type=F32)


def _bdot_nt(a, b):
    return lax.dot_general(a.astype(BF16), b.astype(BF16), (((1,), (1,)), ((), ())),
                           preferred_element_type=F32)


def _mm_kernel(x_ref, w_ref, o_ref):
    o_ref[...] = jnp.dot(x_ref[...], w_ref[...], preferred_element_type=F32).astype(o_ref.dtype)


def _matmul(x, w, layer, out_dtype, tm, tn, name):
    m, k = x.shape
    n = w.shape[2]
    tm, tn = _pick(m, tm), _pick(n, tn)
    return pl.pallas_call(
        _mm_kernel,
        grid=(m // tm, n // tn),
        in_specs=[pl.BlockSpec((tm, k), lambda i, j: (i, 0)),
                  pl.BlockSpec((None, k, tn), lambda i, j: (layer, 0, j))],
        out_specs=pl.BlockSpec((tm, tn), lambda i, j: (i, j)),
        out_shape=jax.ShapeDtypeStruct((m, n), out_dtype),
        compiler_params=_cparams(2),
        name=name,
    )(x, w)


def _mm_nt_kernel(x_ref, w_ref, o_ref):
    o_ref[...] = lax.dot_general(x_ref[...], w_ref[...], (((1,), (1,)), ((), ())),
                                 preferred_element_type=F32).astype(o_ref.dtype)


def _matmul_nt(x, w_t, layer, out_dtype, tm, tn, name):
    m, k = x.shape
    n = w_t.shape[1]
    tm, tn = _pick(m, tm), _pick(n, tn)
    return pl.pallas_call(
        _mm_nt_kernel,
        grid=(m // tm, n // tn),
        in_specs=[pl.BlockSpec((tm, k), lambda i, j: (i, 0)),
                  pl.BlockSpec((None, tn, k), lambda i, j: (layer, j, 0))],
        out_specs=pl.BlockSpec((tm, tn), lambda i, j: (i, j)),
        out_shape=jax.ShapeDtypeStruct((m, n), out_dtype),
        compiler_params=_cparams(2),
        name=name,
    )(x, w_t)


def _mm2_kernel(a_ref, b_ref, wa_ref, wb_ref, o_ref):
    acc = jnp.dot(a_ref[...], wa_ref[...], preferred_element_type=F32)
    acc = acc + jnp.dot(b_ref[...], wb_ref[...], preferred_element_type=F32)
    o_ref[...] = acc


def _matmul2(a, b, w, layer, tm, tn, name):
    m, ka = a.shape
    assert b.shape[1] == ka and w.shape[1] == 2 * ka
    n = w.shape[2]
    tm, tn = _pick(m, tm), _pick(n, tn)
    return pl.pallas_call(
        _mm2_kernel,
        grid=(m // tm, n // tn),
        in_specs=[pl.BlockSpec((tm, ka), lambda i, j: (i, 0)),
                  pl.BlockSpec((tm, ka), lambda i, j: (i, 0)),
                  pl.BlockSpec((None, ka, tn), lambda i, j: (layer, 0, j)),
                  pl.BlockSpec((None, ka, tn), lambda i, j: (layer, 1, j))],
        out_specs=pl.BlockSpec((tm, tn), lambda i, j: (i, j)),
        out_shape=jax.ShapeDtypeStruct((m, n), F32),
        compiler_params=_cparams(2),
        name=name,
    )(a, b, w, w)


def _swiglu_kernel(x_ref, wg_ref, wu_ref, o_ref, wgb_ref, wub_ref):
    @pl.when(pl.program_id(1) == 0)
    def _():
        wgb_ref[...] = wg_ref[...].astype(BF16)
        wub_ref[...] = wu_ref[...].astype(BF16)

    x = x_ref[...]
    g = jnp.dot(x, wgb_ref[...], preferred_element_type=F32)
    u = jnp.dot(x, wub_ref[...], preferred_element_type=F32)
    o_ref[...] = (g * (1.0 / (1.0 + jnp.exp(-g))) * u).astype(o_ref.dtype)


def _swiglu_up(x, w_gate, w_up, layer, tf, tm, name):
    m, k = x.shape
    d_ff = w_gate.shape[2]
    tm = _pick(m, tm)
    w_spec = pl.BlockSpec((None, k, tf), lambda j, i: (layer, 0, j))
    return pl.pallas_call(
        _swiglu_kernel,
        grid=(d_ff // tf, m // tm),
        in_specs=[pl.BlockSpec((tm, k), lambda j, i: (i, 0)), w_spec, w_spec],
        out_specs=pl.BlockSpec((tm, tf), lambda j, i: (i, j)),
        out_shape=jax.ShapeDtypeStruct((m, d_ff), BF16),
        scratch_shapes=[pltpu.VMEM((k, tf), BF16), pltpu.VMEM((k, tf), BF16)],
        compiler_params=_cparams(2),
        name=name,
    )(x, w_gate, w_up)


def _ln_kernel(x_ref, h_ref, g_ref, b_ref, o_ref, ob_ref, *, alpha):
    y = alpha * x_ref[...] + h_ref[...]
    mu = jnp.mean(y, axis=-1, keepdims=True)
    d = y - mu
    var = jnp.mean(d * d, axis=-1, keepdims=True)
    o = d * lax.rsqrt(var + LN_EPS) * g_ref[...] + b_ref[...]
    o_ref[...] = o
    ob_ref[...] = o.astype(BF16)


def _res_ln(x, h, g, b, layer, alpha, row0, rows, name):
    d = x.shape[1]
    tm = _pick(rows, 256)
    assert row0 % tm == 0
    off = row0 // tm
    return pl.pallas_call(
        functools.partial(_ln_kernel, alpha=alpha),
        grid=(rows // tm,),
        in_specs=[pl.BlockSpec((tm, d), lambda i: (i + off, 0)),
                  pl.BlockSpec((tm, d), lambda i: (i + off, 0)),
                  pl.BlockSpec((None, 1, d), lambda i: (layer, 0, 0)),
                  pl.BlockSpec((None, 1, d), lambda i: (layer, 0, 0))],
        out_specs=[pl.BlockSpec((tm, d), lambda i: (i, 0)),
                   pl.BlockSpec((tm, d), lambda i: (i, 0))],
        out_shape=[jax.ShapeDtypeStruct((rows, d), F32),
                   jax.ShapeDtypeStruct((rows, d), BF16)],
        compiler_params=_cparams(1),
        name=name,
    )(x, h, g, b)


def _log_sigmoid(z):
    return jnp.minimum(z, 0.0) - jnp.log(1.0 + jnp.exp(-jnp.abs(z)))


def _split3(x):
    hi = x.astype(BF16)
    r1 = x - hi.astype(F32)
    mid = r1.astype(BF16)
    lo = (r1 - mid.astype(F32)).astype(BF16)
    return jnp.concatenate([hi, mid, lo], axis=1)


def _sum3(y, w):
    return y[:, :w] + y[:, w:2 * w] + y[:, 2 * w:3 * w]


def _iota2(shape):
    return lax.broadcasted_iota(jnp.int32, shape, 0), lax.broadcasted_iota(jnp.int32, shape, 1)


def _idiv(x, n):
    return x >> (n.bit_length() - 1) if n & (n - 1) == 0 else x // n


def _imod(x, n):
    return x & (n - 1) if n & (n - 1) == 0 else x % n


def _ones_where(mask):
    return jnp.where(mask, 1.0, 0.0).astype(BF16)


def _log_gate(gl, wgk, bgk):
    z = jnp.dot(gl.astype(BF16), wgk, preferred_element_type=F32) + bgk
    return _log_sigmoid(z) / GLA_GATE_NORM


def _gla_finish(o, gn, gr):
    ms = jnp.mean(o * o, axis=-1, keepdims=True)
    on = o * lax.rsqrt(ms + RMS_EPS) * gn
    return on * (gr * (1.0 / (1.0 + jnp.exp(-gr))))


def _rows_bcast(x, row, n):
    return jnp.broadcast_to(x[row:row + 1, :], (n, x.shape[1]))


def _gla_prompt_kernel(q_ref, k_ref, v_ref, gl_ref, gr_ref, wgk_ref, bgk_ref, gn_ref,
                       o_ref, sfin_ref, st_ref, *, n_tiles):
    t_blk = pl.program_id(2)

    @pl.when(t_blk == 0)
    def _():
        st_ref[...] = jnp.zeros_like(st_ref)

    n = GLA_TILE
    n_sub = n // GLA_SUB
    r, c = _iota2((n, n))
    tri = _ones_where(c <= r)
    diag_mask = (_idiv(r, GLA_SUB) == _idiv(c, GLA_SUB)) & (c <= r)
    c_sub = lax.broadcasted_iota(jnp.int32, (GLA_SUB, n), 1)
    scale = GLA_DK ** -0.5

    for t in range(n_tiles):
        rows = pl.ds(t * n, n)
        qs = q_ref[rows, :] * scale
        k = k_ref[rows, :]
        v = v_ref[rows, :]
        la = _log_gate(gl_ref[rows, :], wgk_ref[...], bgk_ref[...])
        g = _sum3(jnp.dot(tri, _split3(la), preferred_element_type=F32), GLA_DK)
        g_last = g[n - 1:n, :]

        g_mid = jnp.concatenate(
            [_rows_bcast(g, s * GLA_SUB + GLA_SUB // 2 - 1, GLA_SUB) for s in range(n_sub)], axis=0)
        g_start = jnp.concatenate(
            [jnp.zeros((GLA_SUB, GLA_DK), F32)]
            + [_rows_bcast(g, s * GLA_SUB - 1, GLA_SUB) for s in range(1, n_sub)], axis=0)

        att = jnp.where(diag_mask, _bdot_nt(qs * jnp.exp(g - g_mid), k * jnp.exp(g_mid - g)), 0.0)
        q_off = qs * jnp.exp(g - g_start)
        blocks = [att[:GLA_SUB, :]]
        for s in range(1, n_sub):
            k_off = k * jnp.exp(jnp.minimum(g[s * GLA_SUB - 1:s * GLA_SUB, :] - g, 0.0))
            a = _bdot_nt(q_off[s * GLA_SUB:(s + 1) * GLA_SUB, :], k_off)
            blocks.append(jnp.where(c_sub < s * GLA_SUB, a, att[s * GLA_SUB:(s + 1) * GLA_SUB, :]))
        att = jnp.concatenate(blocks, axis=0)

        st = st_ref[...]
        o = _bdot(att, v) + _bdot_nt(qs * jnp.exp(g), st)
        k_tail = k * jnp.exp(g_last - g)
        st_ref[...] = st * jnp.exp(g_last) + _bdot(v.T, k_tail)
        o_ref[rows, :] = _gla_finish(o, gn_ref[...], gr_ref[rows, :]).astype(o_ref.dtype)

    @pl.when(t_blk == pl.num_programs(2) - 1)
    def _():
        sfin_ref[...] = st_ref[...].T


def _gla_prompt(p, wgk, bgk, gn, layer, batch, seq, name):
    ct = _pick(seq, 512)
    n_t = seq // ct
    w128 = lambda base, h: base // GLA_DK + h
    w256 = lambda base, h: base // GLA_DV + h
    row = lambda b, h, t: b * n_t + t
    return pl.pallas_call(
        functools.partial(_gla_prompt_kernel, n_tiles=ct // GLA_TILE),
        grid=(batch, GLA_HEADS, n_t),
        in_specs=[pl.BlockSpec((ct, GLA_DK), lambda b, h, t: (row(b, h, t), w128(P_GQ, h))),
                  pl.BlockSpec((ct, GLA_DK), lambda b, h, t: (row(b, h, t), w128(P_GK, h))),
                  pl.BlockSpec((ct, GLA_DV), lambda b, h, t: (row(b, h, t), w256(P_GV, h))),
                  pl.BlockSpec((ct, LANES), lambda b, h, t: (row(b, h, t), P_GLOW // LANES)),
                  pl.BlockSpec((ct, GLA_DV), lambda b, h, t: (row(b, h, t), w256(P_GR, h))),
                  pl.BlockSpec((None, LANES, GLA_DK), lambda b, h, t: (layer, 0, h)),
                  pl.BlockSpec((None, 1, GLA_DK), lambda b, h, t: (layer, 0, h)),
                  pl.BlockSpec((None, 1, GLA_DV), lambda b, h, t: (layer, 0, 0))],
        out_specs=[pl.BlockSpec((ct, GLA_DV), lambda b, h, t: (row(b, h, t), h)),
                   pl.BlockSpec((None, None, GLA_DK, GLA_DV), lambda b, h, t: (b, h, 0, 0))],
        out_shape=[jax.ShapeDtypeStruct((batch * seq, GLA_HEADS * GLA_DV), BF16),
                   jax.ShapeDtypeStruct((batch, GLA_HEADS, GLA_DK, GLA_DV), F32)],
        scratch_shapes=[pltpu.VMEM((GLA_DV, GLA_DK), F32)],
        compiler_params=_cparams(3),
        name=name,
    )(p, p, p, p, p, wgk, bgk, gn)


def _gla_sample_kernel(q_ref, k_ref, v_ref, gl_ref, gr_ref, wgk_ref, bgk_ref, gn_ref, s0_ref,
                       o_ref, sfin_ref, oi_ref, bl_ref, kt_ref, vt_ref, qd_ref, *, t_new, n_b):
    n = n_b * t_new
    r, c = _iota2((n, n))
    same = _idiv(r, t_new) == _idiv(c, t_new)
    causal = same & (c <= r)
    tri = _ones_where(causal)
    ones = _ones_where(same)
    scale = GLA_DK ** -0.5

    qs = q_ref[...] * scale
    k = k_ref[...]
    v = v_ref[...]
    la = _log_gate(gl_ref[...], wgk_ref[...], bgk_ref[...])
    la3 = _split3(la)
    b = _sum3(jnp.dot(tri, la3, preferred_element_type=F32), GLA_DK)
    b_last = _sum3(jnp.dot(ones, la3, preferred_element_type=F32), GLA_DK)
    q_dec = qs * jnp.exp(b)
    att = jnp.where(causal, _bdot_nt(q_dec, k * jnp.exp(-b)), 0.0)
    o_intra = _bdot(att, v)

    bl_ref[...] = b_last
    kt_ref[...] = (k * jnp.exp(b_last - b)).astype(BF16)
    vt_ref[...] = v.T.astype(BF16)
    qd_ref[...] = q_dec.astype(BF16)
    oi_ref[...] = jnp.zeros_like(oi_ref)
    row_seq = _idiv(lax.broadcasted_iota(jnp.int32, (n, 1), 0), t_new)

    def body(i, carry):
        st = s0_ref[i].T
        mine = row_seq == i
        o_full = lax.dot_general(qd_ref[...], st.astype(BF16), (((1,), (1,)), ((), ())),
                                 preferred_element_type=F32)
        oi_ref[...] += jnp.where(mine, o_full, 0.0)
        decay = jnp.exp(bl_ref[pl.ds(pl.multiple_of(i * t_new, t_new), t_new), :][0:1, :])
        kt = jnp.where(mine, kt_ref[...], jnp.zeros_like(kt_ref))
        st_new = st * decay + jnp.dot(vt_ref[...], kt, preferred_element_type=F32)
        sfin_ref[i] = st_new.T
        return carry

    lax.fori_loop(0, n_b, body, 0, unroll=True)
    o_ref[...] = _gla_finish(oi_ref[...] + o_intra, gn_ref[...], gr_ref[...]).astype(o_ref.dtype)


def _gla_sample(p, wgk, bgk, gn, s0, layer, row0, dec_b, t_new, name):
    n_b = GLA_TILE // t_new
    n = n_b * t_new
    r0 = row0 // n
    w128 = lambda base, h: base // GLA_DK + h
    w256 = lambda base, h: base // GLA_DV + h
    return pl.pallas_call(
        functools.partial(_gla_sample_kernel, t_new=t_new, n_b=n_b),
        grid=(dec_b // n_b, GLA_HEADS),
        in_specs=[pl.BlockSpec((n, GLA_DK), lambda i, h: (r0 + i, w128(P_GQ, h))),
                  pl.BlockSpec((n, GLA_DK), lambda i, h: (r0 + i, w128(P_GK, h))),
                  pl.BlockSpec((n, GLA_DV), lambda i, h: (r0 + i, w256(P_GV, h))),
                  pl.BlockSpec((n, LANES), lambda i, h: (r0 + i, P_GLOW // LANES)),
                  pl.BlockSpec((n, GLA_DV), lambda i, h: (r0 + i, w256(P_GR, h))),
                  pl.BlockSpec((None, LANES, GLA_DK), lambda i, h: (layer, 0, h)),
                  pl.BlockSpec((None, 1, GLA_DK), lambda i, h: (layer, 0, h)),
                  pl.BlockSpec((None, 1, GLA_DV), lambda i, h: (layer, 0, 0)),
                  pl.BlockSpec((None, n_b, None, GLA_DK, GLA_DV), lambda i, h: (layer, i, h, 0, 0))],
        out_specs=[pl.BlockSpec((n, GLA_DV), lambda i, h: (i, h)),
                   pl.BlockSpec((n_b, None, GLA_DK, GLA_DV), lambda i, h: (i, h, 0, 0))],
        out_shape=[jax.ShapeDtypeStruct((dec_b * t_new, GLA_HEADS * GLA_DV), BF16),
                   jax.ShapeDtypeStruct(s0.shape[1:], F32)],
        scratch_shapes=[pltpu.VMEM((n, GLA_DV), F32),
                        pltpu.VMEM((n, GLA_DK), F32),
                        pltpu.VMEM((n, GLA_DK), BF16),
                        pltpu.VMEM((GLA_DV, n), BF16),
                        pltpu.VMEM((n, GLA_DK), BF16)],
        compiler_params=_cparams(2),
        name=name,
    )(p, p, p, p, p, wgk, bgk, gn, s0)


def _rms(x, g):
    return x * lax.rsqrt(jnp.mean(x * x, axis=-1, keepdims=True) + RMS_EPS) * g


def _mla_prep_kernel(p_ref, qg_ref, kg_ref, wq_ref, wuk_ref, cos_ref, sin_ref,
                     qt_ref, ckv_ref, kpe_ref, kcat_ref, vt_ref):
    tm = p_ref.shape[0]
    nope_w = MLA_HEADS * MLA_NOPE
    rope_w = MLA_HEADS * MLA_ROPE
    cq = p_ref[:, P_CQ:P_CQ + Q_LORA]
    qa = jnp.dot(_rms(cq, qg_ref[...]).astype(BF16), wq_ref[...], preferred_element_type=F32)
    cos = cos_ref[...]
    sin = sin_ref[...]
    low = lax.broadcasted_iota(jnp.int32, cos.shape, 1) < MLA_ROPE
    for j in range(MLA_HEADS // 2):
        x = qa[:, nope_w + j * LANES:nope_w + (j + 1) * LANES]
        xs = qa[:, nope_w + rope_w + j * LANES:nope_w + rope_w + (j + 1) * LANES]
        rot = x * cos + xs * sin
        tails = (jnp.where(low, rot, 0.0), jnp.where(low, pltpu.roll(rot, MLA_ROPE, 1), 0.0))
        for u in range(2):
            h = 2 * j + u
            q_lat = jnp.dot(qa[:, h * MLA_NOPE:(h + 1) * MLA_NOPE].astype(BF16), wuk_ref[h],
                            preferred_element_type=F32)
            q_t = jnp.concatenate([q_lat, tails[u]], axis=1).T.astype(BF16)
            for tb in range(tm // PAGE_SIZE):
                qt_ref[tb, :, h * PAGE_SIZE:(h + 1) * PAGE_SIZE] = q_t[:, tb * PAGE_SIZE:(tb + 1) * PAGE_SIZE]

    ckv = _rms(p_ref[:, P_CKV:P_CKV + KV_LORA], kg_ref[...])
    kk = p_ref[:, P_KPE:P_KPE + LANES] * jnp.where(low, cos, sin)
    kpe = kk + pltpu.roll(kk, MLA_ROPE, 1)
    ckv_ref[...] = ckv
    kpe_ref[...] = kpe[:, :MLA_ROPE]
    kcat_ref[:, :KV_LORA] = ckv.astype(BF16)
    kcat_ref[:, KV_LORA:] = jnp.where(low, kpe, 0.0).astype(BF16)
    vt_ref[...] = ckv.T.astype(BF16)


def _mla_prep(p, qg, kg, wq, wuk, layer, cos, sin, name):
    t = p.shape[0]
    tm = _pick(t, 256)
    assert tm % PAGE_SIZE == 0
    nb = tm // PAGE_SIZE
    return pl.pallas_call(
        _mla_prep_kernel,
        grid=(t // tm,),
        in_specs=[pl.BlockSpec((tm, P_HEAD), lambda i: (i, 0)),
                  pl.BlockSpec((None, 1, Q_LORA), lambda i: (layer, 0, 0)),
                  pl.BlockSpec((None, 1, KV_LORA), lambda i: (layer, 0, 0)),
                  pl.BlockSpec((None,) + wq.shape[1:], lambda i: (layer, 0, 0)),
                  pl.BlockSpec((None,) + wuk.shape[1:], lambda i: (layer, 0, 0, 0)),
                  pl.BlockSpec((tm, LANES), lambda i: (i, 0)),
                  pl.BlockSpec((tm, LANES), lambda i: (i, 0))],
        out_specs=[pl.BlockSpec((nb, QK_PAD, MLA_HEADS * PAGE_SIZE), lambda i: (i, 0, 0)),
                   pl.BlockSpec((tm, KV_LORA), lambda i: (i, 0)),
                   pl.BlockSpec((tm, MLA_ROPE), lambda i: (i, 0)),
                   pl.BlockSpec((tm, QK_PAD), lambda i: (i, 0)),
                   pl.BlockSpec((KV_LORA, tm), lambda i: (0, i))],
        out_shape=[jax.ShapeDtypeStruct((t // PAGE_SIZE, QK_PAD, MLA_HEADS * PAGE_SIZE), BF16),
                   jax.ShapeDtypeStruct((t, KV_LORA), F32),
                   jax.ShapeDtypeStruct((t, MLA_ROPE), F32),
                   jax.ShapeDtypeStruct((t, QK_PAD), BF16),
                   jax.ShapeDtypeStruct((KV_LORA, t), BF16)],
        compiler_params=_cparams(1),
        name=name,
    )(p, qg, kg, wq, wuk, cos, sin)


def _attn_prompt_kernel(qt_ref, k_ref, vt_ref, wuv_ref, o_ref, m_ref, l_ref, acc_ref, *, tq, tk):
    qi = pl.program_id(1)
    ki = pl.program_id(2)
    last_k = (qi * tq + tq - 1) // tk
    cols = MLA_HEADS * tq

    @pl.when(ki == 0)
    def _():
        m_ref[...] = jnp.full_like(m_ref, NEG)
        l_ref[...] = jnp.zeros_like(l_ref)
        acc_ref[...] = jnp.zeros_like(acc_ref)

    def step(masked):
        k = k_ref[...]
        vt = vt_ref[...]
        cw = ATTN_HEAD_GROUP * tq
        if masked:
            key, col = _iota2((tk, cw))
            visible = ki * tk + key <= qi * tq + _imod(col, tq)
        for j in range(cols // cw):
            cs = slice(j * cw, (j + 1) * cw)
            s_t = jnp.dot(k, qt_ref[:, cs], preferred_element_type=F32) * MLA_SCALE
            if masked:
                s_t = jnp.where(visible, s_t, NEG)
            m_old = m_ref[:, cs]
            m_new = jnp.maximum(m_old, jnp.max(s_t, axis=0, keepdims=True))
            corr = jnp.exp(m_old - m_new)
            p_t = jnp.exp(s_t - m_new)
            l_ref[:, cs] = l_ref[:, cs] * corr + jnp.sum(p_t, axis=0, keepdims=True)
            acc_ref[:, cs] = acc_ref[:, cs] * corr + jnp.dot(vt, p_t.astype(BF16),
                                                             preferred_element_type=F32)
            m_ref[:, cs] = m_new

    @pl.when((ki * tk + tk - 1 <= qi * tq) & (ki <= last_k))
    def _():
        step(False)

    @pl.when((ki * tk + tk - 1 > qi * tq) & (ki <= last_k))
    def _():
        step(True)

    @pl.when(ki == last_k)
    def _():
        for h in range(MLA_HEADS):
            cs = slice(h * tq, (h + 1) * tq)
            o_t = (acc_ref[:, cs] / l_ref[:, cs]).astype(BF16)
            o_ref[:, h * MLA_V:(h + 1) * MLA_V] = lax.dot_general(
                o_t, wuv_ref[h], (((0,), (0,)), ((), ())), preferred_element_type=F32).astype(o_ref.dtype)


def _attn_prompt(qt, kcat, vt, wuv, layer, batch, seq, name):
    tq = PAGE_SIZE
    tk = _pick(seq, ATTN_TK)
    nq, nk = seq // tq, seq // tk

    def k_blk(b, qi, ki):
        return b * nk + jnp.minimum(ki, (qi * tq + tq - 1) // tk)

    return pl.pallas_call(
        functools.partial(_attn_prompt_kernel, tq=tq, tk=tk),
        grid=(batch, nq, nk),
        in_specs=[pl.BlockSpec((None, QK_PAD, MLA_HEADS * tq), lambda b, qi, ki: (b * nq + qi, 0, 0)),
                  pl.BlockSpec((tk, QK_PAD), lambda b, qi, ki: (k_blk(b, qi, ki), 0)),
                  pl.BlockSpec((KV_LORA, tk), lambda b, qi, ki: (0, k_blk(b, qi, ki))),
                  pl.BlockSpec((None,) + wuv.shape[1:], lambda b, qi, ki: (layer, 0, 0, 0))],
        out_specs=pl.BlockSpec((tq, MLA_HEADS * MLA_V), lambda b, qi, ki: (b * nq + qi, 0)),
        out_shape=jax.ShapeDtypeStruct((batch * seq, MLA_HEADS * MLA_V), BF16),
        scratch_shapes=[pltpu.VMEM((1, MLA_HEADS * tq), F32),
                        pltpu.VMEM((1, MLA_HEADS * tq), F32),
                        pltpu.VMEM((KV_LORA, MLA_HEADS * tq), F32)],
        compiler_params=_cparams(3),
        name=name,
    )(qt, kcat, vt, wuv)


def _col(row_vec, n):
    return jnp.broadcast_to(row_vec, (n, n)).T[:, 0:1]


def _page_copies(pt_ref, ckv_hbm, kpe_hbm, ck_buf, kp_buf, sem, layer, seq, page0, slot, n_pg):
    copies = []
    for i in range(n_pg):
        page = pt_ref[seq, page0 + i]
        copies.append(pltpu.make_async_copy(ckv_hbm.at[layer, page], ck_buf.at[slot, i], sem.at[slot]))
        copies.append(pltpu.make_async_copy(kpe_hbm.at[layer, page], kp_buf.at[slot, i], sem.at[slot]))
    return copies


def _attn_sample_kernel(pt_ref, qt_ref, ckv_hbm, kpe_hbm, cn_ref, kn_ref, o_ref,
                        ck_buf, kp_buf, sem, m_ref, l_ref, acc_ref, *, layer, t_new, n_pg):
    b, g = pl.program_id(0), pl.program_id(1)
    n_g = pl.num_programs(1)
    step = b * n_g + g
    last = pl.num_programs(0) * n_g - 1
    slot = step % 2
    rows = MLA_HEADS * t_new
    tn = (((0,), (0,)), ((), ()))
    copies = functools.partial(_page_copies, pt_ref, ckv_hbm, kpe_hbm, ck_buf, kp_buf, sem, layer)

    @pl.when(step == 0)
    def _():
        for c in copies(0, 0, 0, n_pg):
            c.start()

    @pl.when(step < last)
    def _():
        nxt = step + 1
        for c in copies(nxt // n_g, (nxt % n_g) * n_pg, 1 - slot, n_pg):
            c.start()

    @pl.when(g == 0)
    def _():
        m_ref[...] = jnp.full_like(m_ref, NEG)
        l_ref[...] = jnp.zeros_like(l_ref)
        acc_ref[...] = jnp.zeros_like(acc_ref)

    def update(s_t, kv):
        m_old = m_ref[...]
        m_new = jnp.maximum(m_old, jnp.max(s_t, axis=0, keepdims=True))
        corr = jnp.exp(m_old - m_new)
        p_t = jnp.exp(s_t - m_new)
        l_ref[...] = l_ref[...] * corr + jnp.sum(p_t, axis=0, keepdims=True)
        m_ref[...] = m_new
        acc_ref[...] = acc_ref[...] * _col(corr, rows) + jnp.dot(p_t.T.astype(BF16), kv,
                                                                 preferred_element_type=F32)

    for c in copies(b, g * n_pg, slot, n_pg):
        c.wait()
    q_lat = qt_ref[:KV_LORA, :]
    q_pe = qt_ref[KV_LORA:KV_LORA + MLA_ROPE, :]
    kv = jnp.concatenate([ck_buf[slot, i].astype(BF16) for i in range(n_pg)], axis=0)
    s_pe = jnp.concatenate(
        [lax.dot_general(kp_buf[slot, i].astype(BF16), q_pe, tn, preferred_element_type=F32)
         for i in range(n_pg)], axis=0)
    update((jnp.dot(kv, q_lat, preferred_element_type=F32) + s_pe) * MLA_SCALE, kv)

    @pl.when(g == n_g - 1)
    def _():
        zc = jnp.zeros((PAGE_SIZE - t_new, KV_LORA), F32)
        zp = jnp.zeros((PAGE_SIZE - t_new, MLA_ROPE), F32)
        key, qrow = _iota2((PAGE_SIZE, rows))
        causal = key <= _imod(qrow, t_new)
        kv_new = jnp.concatenate([cn_ref[...], zc], axis=0).astype(BF16)
        kp_new = jnp.concatenate([kn_ref[...], zp], axis=0).astype(BF16)
        s_t = (jnp.dot(kv_new, q_lat, preferred_element_type=F32)
               + jnp.dot(kp_new, q_pe, preferred_element_type=F32))
        update(jnp.where(causal, s_t * MLA_SCALE, NEG), kv_new)
        o_ref[...] = (acc_ref[...] / _col(l_ref[...], rows)).astype(o_ref.dtype)


def _attn_sample(page_table, q_t, cache_ckv, cache_kpe_t, ckv_new, kpe_new, layer, name):
    dec_b, _, rows = q_t.shape
    t_new = rows // MLA_HEADS
    n_pages = page_table.shape[1]
    n_pg = _pick(n_pages, PAGES_PER_STEP)
    return pl.pallas_call(
        functools.partial(_attn_sample_kernel, layer=layer, t_new=t_new, n_pg=n_pg),
        grid_spec=pltpu.PrefetchScalarGridSpec(
            num_scalar_prefetch=1,
            grid=(dec_b, n_pages // n_pg),
            in_specs=[pl.BlockSpec((None, QK_PAD, rows), lambda b, g, pt: (b, 0, 0)),
                      pl.BlockSpec(memory_space=pl.ANY),
                      pl.BlockSpec(memory_space=pl.ANY),
                      pl.BlockSpec((None, t_new, KV_LORA), lambda b, g, pt: (b, 0, 0)),
                      pl.BlockSpec((None, t_new, MLA_ROPE), lambda b, g, pt: (b, 0, 0))],
            out_specs=pl.BlockSpec((None, rows, KV_LORA), lambda b, g, pt: (b, 0, 0)),
            scratch_shapes=[pltpu.VMEM((2, n_pg, PAGE_SIZE, KV_LORA), F32),
                            pltpu.VMEM((2, n_pg, MLA_ROPE, PAGE_SIZE), F32),
                            pltpu.SemaphoreType.DMA((2,)),
                            pltpu.VMEM((1, rows), F32),
                            pltpu.VMEM((1, rows), F32),
                            pltpu.VMEM((rows, KV_LORA), F32)]),
        out_shape=jax.ShapeDtypeStruct((dec_b, rows, KV_LORA), BF16),
        compiler_params=_cparams(2),
        name=name,
    )(page_table, q_t, cache_ckv, cache_kpe_t, ckv_new, kpe_new)


def _head_out_kernel(x_ref, w_ref, o_ref):
    o_ref[...] = jnp.dot(x_ref[...], w_ref[...], preferred_element_type=F32).astype(o_ref.dtype)


def _head_out(x, wuv, layer, name):
    h, t, c = x.shape
    return pl.pallas_call(
        _head_out_kernel,
        grid=(h,),
        in_specs=[pl.BlockSpec((None, t, c), lambda i: (i, 0, 0)),
                  pl.BlockSpec((None, None, c, MLA_V), lambda i: (layer, i, 0, 0))],
        out_specs=pl.BlockSpec((t, MLA_V), lambda i: (0, i)),
        out_shape=jax.ShapeDtypeStruct((t, h * MLA_V), BF16),
        compiler_params=_cparams(1),
        name=name,
    )(x, wuv)


def _prep_weights(w_in, w_gk, w_uq, w_uk, w_uv):
    depth = w_in.shape[0]
    wt = jnp.swapaxes(w_in, 1, 2)
    s_gq, s_gk, s_gv, s_gl, s_gr, s_cq, s_ckv, s_kpe = jnp.split(
        wt, [1024, 2048, 4096, 4112, 6160, 7056, 7568], axis=1)
    half = MLA_ROPE // 2
    kpe_sw = jnp.concatenate([s_kpe[:, half:], s_kpe[:, :half]], axis=1)
    head = jnp.concatenate([s_cq, s_ckv, s_kpe, kpe_sw, s_gl], axis=1)
    head = jnp.pad(head, ((0, 0), (0, P_HEAD - head.shape[1]), (0, 0)))
    wp_t = jnp.concatenate([head, s_gv, s_gr, s_gq, s_gk], axis=1).astype(BF16)

    wgk = jnp.pad(w_gk, ((0, 0), (0, LANES - GLA_GATE_RANK), (0, 0))).astype(BF16)
    nope = w_uq[..., :MLA_NOPE].reshape(depth, Q_LORA, -1)
    pe = w_uq[..., MLA_NOPE:]
    pe_sw = jnp.concatenate([pe[..., half:], pe[..., :half]], axis=-1)
    wq = jnp.concatenate([nope, pe.reshape(depth, Q_LORA, -1), pe_sw.reshape(depth, Q_LORA, -1)],
                         axis=2).astype(BF16)
    wuk = jnp.transpose(w_uk, (0, 2, 3, 1)).astype(BF16)
    wuv = jnp.transpose(w_uv, (0, 2, 1, 3)).astype(BF16)
    return wp_t, wgk, wq, wuk, wuv


def _rope_tables(pos):
    half = MLA_ROPE // 2
    inv = ROPE_BASE ** (-jnp.arange(half, dtype=F32) / half)
    ang = pos.astype(F32)[:, None] * inv[None, :]
    cos, sin = jnp.cos(ang), jnp.sin(ang)
    return jnp.tile(cos, (1, LANES // half)), jnp.tile(jnp.concatenate([-sin, sin], axis=1), (1, LANES // MLA_ROPE))


def _forward(x_prompt, x_sample, cache_ckv, cache_kpe, state_gla, page_table, w_in, w_gk, b_gk, gla_norm_g,
             q_norm_g, w_uq, kv_norm_g, w_uk, w_uv, w_o, ln1_g, ln1_b, w_ffn_gate, w_ffn_up, w_ffn_down,
             ln2_g, ln2_b):
    batch, seq, d = x_prompt.shape
    dec_b, t_new, _ = x_sample.shape
    depth = w_in.shape[0]
    n_p, n_s = batch * seq, dec_b * t_new
    past = page_table.shape[1] * PAGE_SIZE
    alpha = (2.0 * depth) ** 0.25
    assert n_p % PAGE_SIZE == 0 and n_s % PAGE_SIZE == 0 and PAGE_SIZE % t_new == 0

    pos = jnp.concatenate([jnp.tile(jnp.arange(seq, dtype=jnp.int32), batch),
                           jnp.tile(past + jnp.arange(t_new, dtype=jnp.int32), dec_b)])
    cos, sin = _rope_tables(pos)
    cache_kpe_t = jnp.swapaxes(cache_kpe, 2, 3)

    wp_t, wgk, wq, wuk, wuv = _prep_weights(w_in, w_gk, w_uq, w_uk, w_uv)
    wo, wd = w_o.astype(BF16), w_ffn_down.astype(BF16)
    wg, wu = w_ffn_gate, w_ffn_up
    row = lambda v: v.reshape(depth, 1, -1)
    bgk, gn, qg, kg = row(b_gk), row(gla_norm_g), row(q_norm_g), row(kv_norm_g)
    g1, b1, g2, b2 = row(ln1_g), row(ln1_b), row(ln2_g), row(ln2_b)

    x = jnp.concatenate([x_prompt.reshape(n_p, d), x_sample.reshape(n_s, d)], axis=0)
    xb = x.astype(BF16)
    outs = [[] for _ in range(6)]
    seqs_per_blk = PAGE_SIZE // t_new
    for l in range(depth):
        p = _matmul_nt(xb, wp_t, l, F32, 1024, 512, f"proj{l}")
        og_p, s_p = _gla_prompt(p, wgk, bgk, gn, l, batch, seq, f"gla_prompt{l}")
        og_s, s_s = _gla_sample(p, wgk, bgk, gn, state_gla, l, n_p, dec_b, t_new, f"gla_sample{l}")
        qt, ckv, kpe, kcat, vt = _mla_prep(p, qg, kg, wq, wuk, l, cos, sin, f"mla_prep{l}")

        om_p = _attn_prompt(qt, kcat, vt, wuv, l, batch, seq, f"attn_prompt{l}")
        q_s = qt[n_p // PAGE_SIZE:].reshape(-1, QK_PAD, MLA_HEADS, seqs_per_blk, t_new)
        q_s = q_s.transpose(0, 3, 1, 2, 4).reshape(dec_b, QK_PAD, MLA_HEADS * t_new)
        ckv_s = ckv[n_p:].reshape(dec_b, t_new, KV_LORA)
        kpe_s = kpe[n_p:].reshape(dec_b, t_new, MLA_ROPE)
        ol_s = _attn_sample(page_table, q_s, cache_ckv, cache_kpe_t, ckv_s, kpe_s, l, f"attn_sample{l}")
        ol_s = ol_s.reshape(dec_b, MLA_HEADS, t_new, KV_LORA).transpose(1, 0, 2, 3)
        om_s = _head_out(ol_s.reshape(MLA_HEADS, n_s, KV_LORA), wuv, l, f"head_out{l}")

        o_gla = jnp.concatenate([og_p, og_s], axis=0)
        o_mla = jnp.concatenate([om_p, om_s], axis=0)
        h = _matmul2(o_gla, o_mla, wo, l, 1024, 512, f"out_proj{l}")
        x1, x1b = _res_ln(x, h, g1, b1, l, alpha, 0, n_p + n_s, f"ln1_{l}")
        hh = _swiglu_up(x1b, wg, wu, l, 256, 1024, f"ffn_up{l}")
        f = _matmul(hh, wd, l, F32, 512, 256, f"ffn_down{l}")
        if l + 1 < depth:
            x, xb = _res_ln(x1, f, g2, b2, l, alpha, 0, n_p + n_s, f"ln2_{l}")
        else:
            y_p, _ = _res_ln(x1, f, g2, b2, l, alpha, 0, n_p, f"ln2_{l}p")
            y_s, _ = _res_ln(x1, f, g2, b2, l, alpha, n_p, n_s, f"ln2_{l}s")

        outs[0].append(ckv[:n_p].reshape(batch, seq, KV_LORA))
        outs[1].append(kpe[:n_p].reshape(batch, seq, MLA_ROPE))
        outs[2].append(s_p)
        outs[3].append(ckv_s)
        outs[4].append(kpe_s)
        outs[5].append(s_s)

    return (y_p.reshape(batch, seq, d), y_s.reshape(dec_b, t_new, d)) + tuple(jnp.stack(o) for o in outs)


def kernel(x_prompt, x_sample, cache_ckv, cache_kpe, state_gla, page_table, w_in, w_gk, b_gk, gla_norm_g, q_norm_g, w_uq, kv_norm_g, w_uk, w_uv, w_o, ln1_g, ln1_b, w_ffn_gate, w_ffn_up, w_ffn_down, ln2_g, ln2_b):
    return _forward(x_prompt, x_sample, cache_ckv, cache_kpe, state_gla, page_table, w_in, w_gk, b_gk,
                    gla_norm_g, q_norm_g, w_uq, kv_norm_g, w_uk, w_uv, w_o, ln1_g, ln1_b, w_ffn_gate,
                    w_ffn_up, w_ffn_down, ln2_g, ln2_b)
```

```python
import functools

import jax
import jax.numpy as jnp
from jax import lax
from jax.experimental import pallas as pl
from jax.experimental.pallas import tpu as pltpu

GLA_HEADS = 8
GLA_DK = 128
GLA_DV = 256
GLA_GATE_RANK = 16
GLA_GATE_NORM = 16.0
MLA_HEADS = 16
MLA_NOPE = 128
MLA_ROPE = 64
MLA_V = 128
Q_LORA = 896
KV_LORA = 512
MLA_SCALE = (MLA_NOPE + MLA_ROPE) ** -0.5
ROPE_BASE = 10000.0
PAGE_SIZE = 128
NEG = -1e30
LN_EPS = 1e-5
RMS_EPS = 1e-6

LANES = 128
GLA_SUB = 32
GLA_TILE = 128
GLA_HEAD_GROUP = 4
QK_PAD = KV_LORA + LANES
PAGES_PER_STEP = 32
PAGE_SLOTS = 3
ATTN_HEAD_GROUP = 16
ATTN_TK = 512
VMEM_LIMIT = 56 * 1024 * 1024

P_CQ = 0
P_CKV = 896
P_KPE = 1408
P_GLOW = 1536
P_HEAD = 2048
P_GV = 2048
P_GR = 4096
P_GQ = 6144
P_GK = 7168
P_COLS = 8192

BF16 = jnp.bfloat16
F32 = jnp.float32


def _cparams(n_axes):
    return pltpu.CompilerParams(dimension_semantics=("arbitrary",) * n_axes,
                                vmem_limit_bytes=VMEM_LIMIT)


def _pick(n, pref):
    if n <= pref:
        return n
    t = pref
    while n % t:
        t //= 2
    return t


def _bdot(a, b):
    return jnp.dot(a.astype(BF16), b.astype(BF16), preferred_element_type=F32)


def _bdot_nt(a, b):
    return lax.dot_general(a.astype(BF16), b.astype(BF16), (((1,), (1,)), ((), ())),
                           preferred_element_type=F32)


def _mm_kernel(x_ref, w_ref, o_ref):
    o_ref[...] = jnp.dot(x_ref[...], w_ref[...], preferred_element_type=F32).astype(o_ref.dtype)


def _matmul(x, w, layer, out_dtype, tm, tn, name):
    m, k = x.shape
    n = w.shape[2]
    tm, tn = _pick(m, tm), _pick(n, tn)
    return pl.pallas_call(
        _mm_kernel,
        grid=(m // tm, n // tn),
        in_specs=[pl.BlockSpec((tm, k), lambda i, j: (i, 0)),
                  pl.BlockSpec((None, k, tn), lambda i, j: (layer, 0, j))],
        out_specs=pl.BlockSpec((tm, tn), lambda i, j: (i, j)),
        out_shape=jax.ShapeDtypeStruct((m, n), out_dtype),
        compiler_params=_cparams(2),
        name=name,
    )(x, w)


def _mm_nt_kernel(x_ref, w_ref, o_ref):
    o_ref[...] = lax.dot_general(x_ref[...], w_ref[...], (((1,), (1,)), ((), ())),
                                 preferred_element_type=F32).astype(o_ref.dtype)


def _matmul_nt(x, w_t, layer, out_dtype, tm, tn, name):
    m, k = x.shape
    n = w_t.shape[1]
    tm, tn = _pick(m, tm), _pick(n, tn)
    return pl.pallas_call(
        _mm_nt_kernel,
        grid=(m // tm, n // tn),
        in_specs=[pl.BlockSpec((tm, k), lambda i, j: (i, 0)),
                  pl.BlockSpec((None, tn, k), lambda i, j: (layer, j, 0))],
        out_specs=pl.BlockSpec((tm, tn), lambda i, j: (i, j)),
        out_shape=jax.ShapeDtypeStruct((m, n), out_dtype),
        compiler_params=_cparams(2),
        name=name,
    )(x, w_t)


def _mm2_kernel(a_ref, b_ref, wa_ref, wb_ref, o_ref):
    acc = jnp.dot(a_ref[...], wa_ref[...], preferred_element_type=F32)
    acc = acc + jnp.dot(b_ref[...], wb_ref[...], preferred_element_type=F32)
    o_ref[...] = acc


def _matmul2(a, b, w, layer, tm, tn, name):
    m, ka = a.shape
    assert b.shape[1] == ka and w.shape[1] == 2 * ka
    n = w.shape[2]
    tm, tn = _pick(m, tm), _pick(n, tn)
    return pl.pallas_call(
        _mm2_kernel,
        grid=(m // tm, n // tn),
        in_specs=[pl.BlockSpec((tm, ka), lambda i, j: (i, 0)),
                  pl.BlockSpec((tm, ka), lambda i, j: (i, 0)),
                  pl.BlockSpec((None, ka, tn), lambda i, j: (layer, 0, j)),
                  pl.BlockSpec((None, ka, tn), lambda i, j: (layer, 1, j))],
        out_specs=pl.BlockSpec((tm, tn), lambda i, j: (i, j)),
        out_shape=jax.ShapeDtypeStruct((m, n), F32),
        compiler_params=_cparams(2),
        name=name,
    )(a, b, w, w)


def _swiglu_kernel(x_ref, wg_ref, wu_ref, o_ref, wgb_ref, wub_ref):
    @pl.when(pl.program_id(1) == 0)
    def _():
        wgb_ref[...] = wg_ref[...].astype(BF16)
        wub_ref[...] = wu_ref[...].astype(BF16)

    x = x_ref[...]
    g = jnp.dot(x, wgb_ref[...], preferred_element_type=F32)
    u = jnp.dot(x, wub_ref[...], preferred_element_type=F32)
    o_ref[...] = (g * (1.0 / (1.0 + jnp.exp(-g))) * u).astype(o_ref.dtype)


def _swiglu_up(x, w_gate, w_up, layer, tf, tm, name):
    m, k = x.shape
    d_ff = w_gate.shape[2]
    tm = _pick(m, tm)
    w_spec = pl.BlockSpec((None, k, tf), lambda j, i: (layer, 0, j))
    return pl.pallas_call(
        _swiglu_kernel,
        grid=(d_ff // tf, m // tm),
        in_specs=[pl.BlockSpec((tm, k), lambda j, i: (i, 0)), w_spec, w_spec],
        out_specs=pl.BlockSpec((tm, tf), lambda j, i: (i, j)),
        out_shape=jax.ShapeDtypeStruct((m, d_ff), BF16),
        scratch_shapes=[pltpu.VMEM((k, tf), BF16), pltpu.VMEM((k, tf), BF16)],
        compiler_params=_cparams(2),
        name=name,
    )(x, w_gate, w_up)


def _ln_kernel(x_ref, h_ref, g_ref, b_ref, o_ref, ob_ref, *, alpha):
    y = alpha * x_ref[...] + h_ref[...]
    mu = jnp.mean(y, axis=-1, keepdims=True)
    d = y - mu
    var = jnp.mean(d * d, axis=-1, keepdims=True)
    o = d * lax.rsqrt(var + LN_EPS) * g_ref[...] + b_ref[...]
    o_ref[...] = o
    ob_ref[...] = o.astype(BF16)


def _res_ln(x, h, g, b, layer, alpha, row0, rows, name):
    d = x.shape[1]
    tm = _pick(rows, 256)
    assert row0 % tm == 0
    off = row0 // tm
    return pl.pallas_call(
        functools.partial(_ln_kernel, alpha=alpha),
        grid=(rows // tm,),
        in_specs=[pl.BlockSpec((tm, d), lambda i: (i + off, 0)),
                  pl.BlockSpec((tm, d), lambda i: (i + off, 0)),
                  pl.BlockSpec((None, 1, d), lambda i: (layer, 0, 0)),
                  pl.BlockSpec((None, 1, d), lambda i: (layer, 0, 0))],
        out_specs=[pl.BlockSpec((tm, d), lambda i: (i, 0)),
                   pl.BlockSpec((tm, d), lambda i: (i, 0))],
        out_shape=[jax.ShapeDtypeStruct((rows, d), F32),
                   jax.ShapeDtypeStruct((rows, d), BF16)],
        compiler_params=_cparams(1),
        name=name,
    )(x, h, g, b)


def _log_sigmoid(z):
    return jnp.minimum(z, 0.0) - jnp.log(1.0 + jnp.exp(-jnp.abs(z)))


def _split3(x):
    hi = x.astype(BF16)
    r1 = x - hi.astype(F32)
    mid = r1.astype(BF16)
    lo = (r1 - mid.astype(F32)).astype(BF16)
    return jnp.concatenate([hi, mid, lo], axis=1)


def _sum3(y, w):
    return y[:, :w] + y[:, w:2 * w] + y[:, 2 * w:3 * w]


def _iota2(shape):
    return lax.broadcasted_iota(jnp.int32, shape, 0), lax.broadcasted_iota(jnp.int32, shape, 1)


def _idiv(x, n):
    return x >> (n.bit_length() - 1) if n & (n - 1) == 0 else x // n


def _imod(x, n):
    return x & (n - 1) if n & (n - 1) == 0 else x % n


def _ones_where(mask):
    return jnp.where(mask, 1.0, 0.0).astype(BF16)


def _log_gate(gl, wgk, bgk):
    z = jnp.dot(gl.astype(BF16), wgk, preferred_element_type=F32) + bgk
    return _log_sigmoid(z) / GLA_GATE_NORM


def _gla_finish(o, gn, gr):
    ms = jnp.mean(o * o, axis=-1, keepdims=True)
    on = o * lax.rsqrt(ms + RMS_EPS) * gn
    return on * (gr * (1.0 / (1.0 + jnp.exp(-gr))))


def _rows_bcast(x, row, n):
    return jnp.broadcast_to(x[row:row + 1, :], (n, x.shape[1]))


def _gla_prompt_kernel(q_ref, k_ref, v_ref, gl_ref, gr_ref, wgk_ref, bgk_ref, gn_ref,
                       o_ref, sfin_ref, st_ref, *, n_tiles):
    t_blk = pl.program_id(2)
    dk, dv = GLA_DK, GLA_DV

    @pl.when(t_blk == 0)
    def _():
        st_ref[...] = jnp.zeros_like(st_ref)

    n = GLA_TILE
    n_sub = n // GLA_SUB
    r, c = _iota2((n, n))
    tri = _ones_where(c <= r)
    diag_mask = (_idiv(r, GLA_SUB) == _idiv(c, GLA_SUB)) & (c <= r)
    c_sub = lax.broadcasted_iota(jnp.int32, (GLA_SUB, n), 1)
    scale = GLA_DK ** -0.5

    for t, u in [(t, u) for t in range(n_tiles) for u in range(GLA_HEAD_GROUP)]:
        rows = pl.ds(t * n, n)
        kc = slice(u * dk, (u + 1) * dk)
        vc = slice(u * dv, (u + 1) * dv)
        qs = q_ref[rows, kc] * scale
        k = k_ref[rows, kc]
        v = v_ref[rows, vc]
        la = _log_gate(gl_ref[rows, :], wgk_ref[:, kc], bgk_ref[:, kc])
        g = _sum3(jnp.dot(tri, _split3(la), preferred_element_type=F32), GLA_DK)
        g_last = g[n - 1:n, :]

        g_mid = jnp.concatenate(
            [_rows_bcast(g, s * GLA_SUB + GLA_SUB // 2 - 1, GLA_SUB) for s in range(n_sub)], axis=0)
        g_start = jnp.concatenate(
            [jnp.zeros((GLA_SUB, GLA_DK), F32)]
            + [_rows_bcast(g, s * GLA_SUB - 1, GLA_SUB) for s in range(1, n_sub)], axis=0)

        att = jnp.where(diag_mask, _bdot_nt(qs * jnp.exp(g - g_mid), k * jnp.exp(g_mid - g)), 0.0)
        q_off = qs * jnp.exp(g - g_start)
        blocks = [att[:GLA_SUB, :]]
        for s in range(1, n_sub):
            k_off = k * jnp.exp(jnp.minimum(g[s * GLA_SUB - 1:s * GLA_SUB, :] - g, 0.0))
            a = _bdot_nt(q_off[s * GLA_SUB:(s + 1) * GLA_SUB, :], k_off)
            blocks.append(jnp.where(c_sub < s * GLA_SUB, a, att[s * GLA_SUB:(s + 1) * GLA_SUB, :]))
        att = jnp.concatenate(blocks, axis=0)

        st = st_ref[u]
        o = _bdot(att, v) + _bdot_nt(qs * jnp.exp(g), st)
        k_tail = k * jnp.exp(g_last - g)
        st_ref[u] = st * jnp.exp(g_last) + _bdot(v.T, k_tail)
        o_ref[rows, vc] = _gla_finish(o, gn_ref[...], gr_ref[rows, vc]).astype(o_ref.dtype)

    @pl.when(t_blk == pl.num_programs(2) - 1)
    def _():
        for u in range(GLA_HEAD_GROUP):
            sfin_ref[u] = st_ref[u].T


def _gla_prompt(p, wgk, bgk, gn, layer, batch, seq, name):
    ct = _pick(seq, 512)
    n_t = seq // ct
    hg = GLA_HEAD_GROUP
    kw, vw = hg * GLA_DK, hg * GLA_DV
    row = lambda b, h, t: b * n_t + t
    return pl.pallas_call(
        functools.partial(_gla_prompt_kernel, n_tiles=ct // GLA_TILE),
        grid=(batch, GLA_HEADS // hg, n_t),
        in_specs=[pl.BlockSpec((ct, kw), lambda b, h, t: (row(b, h, t), P_GQ // kw + h)),
                  pl.BlockSpec((ct, kw), lambda b, h, t: (row(b, h, t), P_GK // kw + h)),
                  pl.BlockSpec((ct, vw), lambda b, h, t: (row(b, h, t), P_GV // vw + h)),
                  pl.BlockSpec((ct, LANES), lambda b, h, t: (row(b, h, t), P_GLOW // LANES)),
                  pl.BlockSpec((ct, vw), lambda b, h, t: (row(b, h, t), P_GR // vw + h)),
                  pl.BlockSpec((None, LANES, kw), lambda b, h, t: (layer, 0, h)),
                  pl.BlockSpec((None, 1, kw), lambda b, h, t: (layer, 0, h)),
                  pl.BlockSpec((None, 1, GLA_DV), lambda b, h, t: (layer, 0, 0))],
        out_specs=[pl.BlockSpec((ct, vw), lambda b, h, t: (row(b, h, t), h)),
                   pl.BlockSpec((None, hg, GLA_DK, GLA_DV), lambda b, h, t: (b, h, 0, 0))],
        out_shape=[jax.ShapeDtypeStruct((batch * seq, GLA_HEADS * GLA_DV), BF16),
                   jax.ShapeDtypeStruct((batch, GLA_HEADS, GLA_DK, GLA_DV), F32)],
        scratch_shapes=[pltpu.VMEM((hg, GLA_DV, GLA_DK), F32)],
        compiler_params=_cparams(3),
        name=name,
    )(p, p, p, p, p, wgk, bgk, gn)


def _gla_sample_kernel(q_ref, k_ref, v_ref, gl_ref, gr_ref, wgk_ref, bgk_ref, gn_ref, s0_ref,
                       o_ref, sfin_ref, oi_ref, bl_ref, kt_ref, vt_ref, qd_ref, *, t_new, n_b):
    n = n_b * t_new
    r, c = _iota2((n, n))
    same = _idiv(r, t_new) == _idiv(c, t_new)
    causal = same & (c <= r)
    tri = _ones_where(causal)
    ones = _ones_where(same)
    scale = GLA_DK ** -0.5

    qs = q_ref[...] * scale
    k = k_ref[...]
    v = v_ref[...]
    la = _log_gate(gl_ref[...], wgk_ref[...], bgk_ref[...])
    la3 = _split3(la)
    b = _sum3(jnp.dot(tri, la3, preferred_element_type=F32), GLA_DK)
    b_last = _sum3(jnp.dot(ones, la3, preferred_element_type=F32), GLA_DK)
    q_dec = qs * jnp.exp(b)
    att = jnp.where(causal, _bdot_nt(q_dec, k * jnp.exp(-b)), 0.0)
    o_intra = _bdot(att, v)

    bl_ref[...] = b_last
    kt_ref[...] = (k * jnp.exp(b_last - b)).astype(BF16)
    vt_ref[...] = v.T.astype(BF16)
    qd_ref[...] = q_dec.astype(BF16)
    oi_ref[...] = jnp.zeros_like(oi_ref)
    row_seq = _idiv(lax.broadcasted_iota(jnp.int32, (n, 1), 0), t_new)

    def body(i, carry):
        st = s0_ref[i].T
        mine = row_seq == i
        o_full = lax.dot_general(qd_ref[...], st.astype(BF16), (((1,), (1,)), ((), ())),
                                 preferred_element_type=F32)
        oi_ref[...] += jnp.where(mine, o_full, 0.0)
        decay = jnp.exp(bl_ref[pl.ds(pl.multiple_of(i * t_new, t_new), t_new), :][0:1, :])
        kt = jnp.where(mine, kt_ref[...], jnp.zeros_like(kt_ref))
        st_new = st * decay + jnp.dot(vt_ref[...], kt, preferred_element_type=F32)
        sfin_ref[i] = st_new.T
        return carry

    lax.fori_loop(0, n_b, body, 0, unroll=True)
    o_ref[...] = _gla_finish(oi_ref[...] + o_intra, gn_ref[...], gr_ref[...]).astype(o_ref.dtype)


def _gla_sample(p, wgk, bgk, gn, s0, layer, row0, dec_b, t_new, name):
    n_b = GLA_TILE // t_new
    n = n_b * t_new
    r0 = row0 // n
    w128 = lambda base, h: base // GLA_DK + h
    w256 = lambda base, h: base // GLA_DV + h
    return pl.pallas_call(
        functools.partial(_gla_sample_kernel, t_new=t_new, n_b=n_b),
        grid=(dec_b // n_b, GLA_HEADS),
        in_specs=[pl.BlockSpec((n, GLA_DK), lambda i, h: (r0 + i, w128(P_GQ, h))),
                  pl.BlockSpec((n, GLA_DK), lambda i, h: (r0 + i, w128(P_GK, h))),
                  pl.BlockSpec((n, GLA_DV), lambda i, h: (r0 + i, w256(P_GV, h))),
                  pl.BlockSpec((n, LANES), lambda i, h: (r0 + i, P_GLOW // LANES)),
                  pl.BlockSpec((n, GLA_DV), lambda i, h: (r0 + i, w256(P_GR, h))),
                  pl.BlockSpec((None, LANES, GLA_DK), lambda i, h: (layer, 0, h)),
                  pl.BlockSpec((None, 1, GLA_DK), lambda i, h: (layer, 0, h)),
                  pl.BlockSpec((None, 1, GLA_DV), lambda i, h: (layer, 0, 0)),
                  pl.BlockSpec((None, n_b, None, GLA_DK, GLA_DV), lambda i, h: (layer, i, h, 0, 0))],
        out_specs=[pl.BlockSpec((n, GLA_DV), lambda i, h: (i, h)),
                   pl.BlockSpec((n_b, None, GLA_DK, GLA_DV), lambda i, h: (i, h, 0, 0))],
        out_shape=[jax.ShapeDtypeStruct((dec_b * t_new, GLA_HEADS * GLA_DV), BF16),
                   jax.ShapeDtypeStruct(s0.shape[1:], F32)],
        scratch_shapes=[pltpu.VMEM((n, GLA_DV), F32),
                        pltpu.VMEM((n, GLA_DK), F32),
                        pltpu.VMEM((n, GLA_DK), BF16),
                        pltpu.VMEM((GLA_DV, n), BF16),
                        pltpu.VMEM((n, GLA_DK), BF16)],
        compiler_params=_cparams(2),
        name=name,
    )(p, p, p, p, p, wgk, bgk, gn, s0)


def _rms(x, g):
    return x * lax.rsqrt(jnp.mean(x * x, axis=-1, keepdims=True) + RMS_EPS) * g


def _mla_prep_kernel(p_ref, qg_ref, kg_ref, wq_ref, wuk_ref, cos_ref, sin_ref,
                     qt_ref, ckv_ref, kpe_ref, kcat_ref, vt_ref):
    tm = p_ref.shape[0]
    nope_w = MLA_HEADS * MLA_NOPE
    rope_w = MLA_HEADS * MLA_ROPE
    cq = p_ref[:, P_CQ:P_CQ + Q_LORA]
    qa = jnp.dot(_rms(cq, qg_ref[...]).astype(BF16), wq_ref[...], preferred_element_type=F32)
    cos = cos_ref[...]
    sin = sin_ref[...]
    low = lax.broadcasted_iota(jnp.int32, cos.shape, 1) < MLA_ROPE
    for j in range(MLA_HEADS // 2):
        x = qa[:, nope_w + j * LANES:nope_w + (j + 1) * LANES]
        xs = qa[:, nope_w + rope_w + j * LANES:nope_w + rope_w + (j + 1) * LANES]
        rot = x * cos + xs * sin
        tails = (jnp.where(low, rot, 0.0), jnp.where(low, pltpu.roll(rot, MLA_ROPE, 1), 0.0))
        for u in range(2):
            h = 2 * j + u
            q_lat = jnp.dot(qa[:, h * MLA_NOPE:(h + 1) * MLA_NOPE].astype(BF16), wuk_ref[h],
                            preferred_element_type=F32)
            q_t = jnp.concatenate([q_lat, tails[u]], axis=1).T.astype(BF16)
            for tb in range(tm // PAGE_SIZE):
                qt_ref[tb, :, h * PAGE_SIZE:(h + 1) * PAGE_SIZE] = q_t[:, tb * PAGE_SIZE:(tb + 1) * PAGE_SIZE]

    ckv = _rms(p_ref[:, P_CKV:P_CKV + KV_LORA], kg_ref[...])
    kk = p_ref[:, P_KPE:P_KPE + LANES] * jnp.where(low, cos, sin)
    kpe = kk + pltpu.roll(kk, MLA_ROPE, 1)
    ckv_ref[...] = ckv
    kpe_ref[...] = kpe[:, :MLA_ROPE]
    kcat_ref[:, :KV_LORA] = ckv.astype(BF16)
    kcat_ref[:, KV_LORA:] = jnp.where(low, kpe, 0.0).astype(BF16)
    vt_ref[...] = ckv.T.astype(BF16)


def _mla_prep(p, qg, kg, wq, wuk, layer, cos, sin, name):
    t = p.shape[0]
    tm = _pick(t, 256)
    assert tm % PAGE_SIZE == 0
    nb = tm // PAGE_SIZE
    return pl.pallas_call(
        _mla_prep_kernel,
        grid=(t // tm,),
        in_specs=[pl.BlockSpec((tm, P_HEAD), lambda i: (i, 0)),
                  pl.BlockSpec((None, 1, Q_LORA), lambda i: (layer, 0, 0)),
                  pl.BlockSpec((None, 1, KV_LORA), lambda i: (layer, 0, 0)),
                  pl.BlockSpec((None,) + wq.shape[1:], lambda i: (layer, 0, 0)),
                  pl.BlockSpec((None,) + wuk.shape[1:], lambda i: (layer, 0, 0, 0)),
                  pl.BlockSpec((tm, LANES), lambda i: (i, 0)),
                  pl.BlockSpec((tm, LANES), lambda i: (i, 0))],
        out_specs=[pl.BlockSpec((nb, QK_PAD, MLA_HEADS * PAGE_SIZE), lambda i: (i, 0, 0)),
                   pl.BlockSpec((tm, KV_LORA), lambda i: (i, 0)),
                   pl.BlockSpec((tm, MLA_ROPE), lambda i: (i, 0)),
                   pl.BlockSpec((tm, QK_PAD), lambda i: (i, 0)),
                   pl.BlockSpec((KV_LORA, tm), lambda i: (0, i))],
        out_shape=[jax.ShapeDtypeStruct((t // PAGE_SIZE, QK_PAD, MLA_HEADS * PAGE_SIZE), BF16),
                   jax.ShapeDtypeStruct((t, KV_LORA), F32),
                   jax.ShapeDtypeStruct((t, MLA_ROPE), F32),
                   jax.ShapeDtypeStruct((t, QK_PAD), BF16),
                   jax.ShapeDtypeStruct((KV_LORA, t), BF16)],
        compiler_params=_cparams(1),
        name=name,
    )(p, qg, kg, wq, wuk, cos, sin)


def _attn_prompt_kernel(qt_ref, k_ref, vt_ref, wuv_ref, o_ref, m_ref, l_ref, acc_ref, *, tq, tk):
    qi = pl.program_id(1)
    ki = pl.program_id(2)
    last_k = (qi * tq + tq - 1) // tk
    cols = MLA_HEADS * tq

    @pl.when(ki == 0)
    def _():
        m_ref[...] = jnp.full_like(m_ref, NEG)
        l_ref[...] = jnp.zeros_like(l_ref)
        acc_ref[...] = jnp.zeros_like(acc_ref)

    def step(masked):
        k = k_ref[...]
        vt = vt_ref[...]
        cw = ATTN_HEAD_GROUP * tq
        if masked:
            key, col = _iota2((tk, cw))
            visible = ki * tk + key <= qi * tq + _imod(col, tq)
        for j in range(cols // cw):
            cs = slice(j * cw, (j + 1) * cw)
            s_t = jnp.dot(k, qt_ref[:, cs], preferred_element_type=F32) * MLA_SCALE
            if masked:
                s_t = jnp.where(visible, s_t, NEG)
            m_old = m_ref[:, cs]
            m_new = jnp.maximum(m_old, jnp.max(s_t, axis=0, keepdims=True))
            corr = jnp.exp(m_old - m_new)
            p_t = jnp.exp(s_t - m_new)
            l_ref[:, cs] = l_ref[:, cs] * corr + jnp.sum(p_t, axis=0, keepdims=True)
            acc_ref[:, cs] = acc_ref[:, cs] * corr + jnp.dot(vt, p_t.astype(BF16),
                                                             preferred_element_type=F32)
            m_ref[:, cs] = m_new

    @pl.when((ki * tk + tk - 1 <= qi * tq) & (ki <= last_k))
    def _():
        step(False)

    @pl.when((ki * tk + tk - 1 > qi * tq) & (ki <= last_k))
    def _():
        step(True)

    @pl.when(ki == last_k)
    def _():
        for h in range(MLA_HEADS):
            cs = slice(h * tq, (h + 1) * tq)
            o_t = (acc_ref[:, cs] / l_ref[:, cs]).astype(BF16)
            o_ref[:, h * MLA_V:(h + 1) * MLA_V] = lax.dot_general(
                o_t, wuv_ref[h], (((0,), (0,)), ((), ())), preferred_element_type=F32).astype(o_ref.dtype)


def _attn_prompt(qt, kcat, vt, wuv, layer, batch, seq, name):
    tq = PAGE_SIZE
    tk = _pick(seq, ATTN_TK)
    nq, nk = seq // tq, seq // tk

    def k_blk(b, qi, ki):
        return b * nk + jnp.minimum(ki, (qi * tq + tq - 1) // tk)

    return pl.pallas_call(
        functools.partial(_attn_prompt_kernel, tq=tq, tk=tk),
        grid=(batch, nq, nk),
        in_specs=[pl.BlockSpec((None, QK_PAD, MLA_HEADS * tq), lambda b, qi, ki: (b * nq + qi, 0, 0)),
                  pl.BlockSpec((tk, QK_PAD), lambda b, qi, ki: (k_blk(b, qi, ki), 0)),
                  pl.BlockSpec((KV_LORA, tk), lambda b, qi, ki: (0, k_blk(b, qi, ki))),
                  pl.BlockSpec((None,) + wuv.shape[1:], lambda b, qi, ki: (layer, 0, 0, 0))],
        out_specs=pl.BlockSpec((tq, MLA_HEADS * MLA_V), lambda b, qi, ki: (b * nq + qi, 0)),
        out_shape=jax.ShapeDtypeStruct((batch * seq, MLA_HEADS * MLA_V), BF16),
        scratch_shapes=[pltpu.VMEM((1, MLA_HEADS * tq), F32),
                        pltpu.VMEM((1, MLA_HEADS * tq), F32),
                        pltpu.VMEM((KV_LORA, MLA_HEADS * tq), F32)],
        compiler_params=_cparams(3),
        name=name,
    )(qt, kcat, vt, wuv)


def _col(row_vec, n):
    return jnp.broadcast_to(row_vec, (n, n)).T[:, 0:1]


def _page_copies(pt_ref, ckv_hbm, kpe_hbm, ck_buf, kp_buf, sem, layer, seq, page0, slot, n_pg):
    copies = []
    for i in range(n_pg):
        page = pt_ref[seq, page0 + i]
        copies.append(pltpu.make_async_copy(ckv_hbm.at[layer, page], ck_buf.at[slot, i], sem.at[slot]))
        copies.append(pltpu.make_async_copy(kpe_hbm.at[layer, page], kp_buf.at[slot, i], sem.at[slot]))
    return copies


def _attn_sample_kernel(pt_ref, qt_ref, ckv_hbm, kpe_hbm, cn_ref, kn_ref, o_ref,
                        ck_buf, kp_buf, sem, m_ref, l_ref, acc_ref, *, layer, t_new, n_pg):
    b, g = pl.program_id(0), pl.program_id(1)
    n_g = pl.num_programs(1)
    step = b * n_g + g
    last = pl.num_programs(0) * n_g - 1
    ahead = PAGE_SLOTS - 1
    slot = step % PAGE_SLOTS
    rows = MLA_HEADS * t_new
    tn = (((0,), (0,)), ((), ()))
    copies = functools.partial(_page_copies, pt_ref, ckv_hbm, kpe_hbm, ck_buf, kp_buf, sem, layer)

    def start_step(t):
        for c in copies(t // n_g, (t % n_g) * n_pg, t % PAGE_SLOTS, n_pg):
            c.start()

    for t in range(ahead):
        @pl.when((step == 0) & (t <= last))
        def _():
            start_step(t)

    @pl.when(step + ahead <= last)
    def _():
        start_step(step + ahead)

    @pl.when(g == 0)
    def _():
        m_ref[...] = jnp.full_like(m_ref, NEG)
        l_ref[...] = jnp.zeros_like(l_ref)
        acc_ref[...] = jnp.zeros_like(acc_ref)

    def update(s_t, kv):
        m_old = m_ref[...]
        m_new = jnp.maximum(m_old, jnp.max(s_t, axis=0, keepdims=True))
        corr = jnp.exp(m_old - m_new)
        p_t = jnp.exp(s_t - m_new)
        l_ref[...] = l_ref[...] * corr + jnp.sum(p_t, axis=0, keepdims=True)
        m_ref[...] = m_new
        acc_ref[...] = acc_ref[...] * _col(corr, rows) + jnp.dot(p_t.T.astype(BF16), kv,
                                                                 preferred_element_type=F32)

    for c in copies(b, g * n_pg, slot, n_pg):
        c.wait()
    q_lat = qt_ref[:KV_LORA, :]
    q_pe = qt_ref[KV_LORA:KV_LORA + MLA_ROPE, :]
    kv = jnp.concatenate([ck_buf[slot, i].astype(BF16) for i in range(n_pg)], axis=0)
    s_pe = jnp.concatenate(
        [lax.dot_general(kp_buf[slot, i].astype(BF16), q_pe, tn, preferred_element_type=F32)
         for i in range(n_pg)], axis=0)
    update((jnp.dot(kv, q_lat, preferred_element_type=F32) + s_pe) * MLA_SCALE, kv)

    @pl.when(g == n_g - 1)
    def _():
        zc = jnp.zeros((PAGE_SIZE - t_new, KV_LORA), F32)
        zp = jnp.zeros((PAGE_SIZE - t_new, MLA_ROPE), F32)
        key, qrow = _iota2((PAGE_SIZE, rows))
        causal = key <= _imod(qrow, t_new)
        kv_new = jnp.concatenate([cn_ref[...], zc], axis=0).astype(BF16)
        kp_new = jnp.concatenate([kn_ref[...], zp], axis=0).astype(BF16)
        s_t = (jnp.dot(kv_new, q_lat, preferred_element_type=F32)
               + jnp.dot(kp_new, q_pe, preferred_element_type=F32))
        update(jnp.where(causal, s_t * MLA_SCALE, NEG), kv_new)
        o_ref[...] = (acc_ref[...] / _col(l_ref[...], rows)).astype(o_ref.dtype)


def _attn_sample(page_table, q_t, cache_ckv, cache_kpe_t, ckv_new, kpe_new, layer, name):
    dec_b, _, rows = q_t.shape
    t_new = rows // MLA_HEADS
    n_pages = page_table.shape[1]
    n_pg = _pick(n_pages, PAGES_PER_STEP)
    return pl.pallas_call(
        functools.partial(_attn_sample_kernel, layer=layer, t_new=t_new, n_pg=n_pg),
        grid_spec=pltpu.PrefetchScalarGridSpec(
            num_scalar_prefetch=1,
            grid=(dec_b, n_pages // n_pg),
            in_specs=[pl.BlockSpec((None, QK_PAD, rows), lambda b, g, pt: (b, 0, 0)),
                      pl.BlockSpec(memory_space=pl.ANY),
                      pl.BlockSpec(memory_space=pl.ANY),
                      pl.BlockSpec((None, t_new, KV_LORA), lambda b, g, pt: (b, 0, 0)),
                      pl.BlockSpec((None, t_new, MLA_ROPE), lambda b, g, pt: (b, 0, 0))],
            out_specs=pl.BlockSpec((None, rows, KV_LORA), lambda b, g, pt: (b, 0, 0)),
            scratch_shapes=[pltpu.VMEM((PAGE_SLOTS, n_pg, PAGE_SIZE, KV_LORA), F32),
                            pltpu.VMEM((PAGE_SLOTS, n_pg, MLA_ROPE, PAGE_SIZE), F32),
                            pltpu.SemaphoreType.DMA((PAGE_SLOTS,)),
                            pltpu.VMEM((1, rows), F32),
                            pltpu.VMEM((1, rows), F32),
                            pltpu.VMEM((rows, KV_LORA), F32)]),
        out_shape=jax.ShapeDtypeStruct((dec_b, rows, KV_LORA), BF16),
        compiler_params=_cparams(2),
        name=name,
    )(page_table, q_t, cache_ckv, cache_kpe_t, ckv_new, kpe_new)


def _head_out_kernel(x_ref, w_ref, o_ref):
    o_ref[...] = jnp.dot(x_ref[...], w_ref[...], preferred_element_type=F32).astype(o_ref.dtype)


def _head_out(x, wuv, layer, name):
    h, t, c = x.shape
    return pl.pallas_call(
        _head_out_kernel,
        grid=(h,),
        in_specs=[pl.BlockSpec((None, t, c), lambda i: (i, 0, 0)),
                  pl.BlockSpec((None, None, c, MLA_V), lambda i: (layer, i, 0, 0))],
        out_specs=pl.BlockSpec((t, MLA_V), lambda i: (0, i)),
        out_shape=jax.ShapeDtypeStruct((t, h * MLA_V), BF16),
        compiler_params=_cparams(1),
        name=name,
    )(x, wuv)


def _prep_weights(w_in, w_gk, w_uq, w_uk, w_uv):
    depth = w_in.shape[0]
    wt = jnp.swapaxes(w_in, 1, 2)
    s_gq, s_gk, s_gv, s_gl, s_gr, s_cq, s_ckv, s_kpe = jnp.split(
        wt, [1024, 2048, 4096, 4112, 6160, 7056, 7568], axis=1)
    half = MLA_ROPE // 2
    kpe_sw = jnp.concatenate([s_kpe[:, half:], s_kpe[:, :half]], axis=1)
    head = jnp.concatenate([s_cq, s_ckv, s_kpe, kpe_sw, s_gl], axis=1)
    head = jnp.pad(head, ((0, 0), (0, P_HEAD - head.shape[1]), (0, 0)))
    wp_t = jnp.concatenate([head, s_gv, s_gr, s_gq, s_gk], axis=1).astype(BF16)

    wgk = jnp.pad(w_gk, ((0, 0), (0, LANES - GLA_GATE_RANK), (0, 0))).astype(BF16)
    nope = w_uq[..., :MLA_NOPE].reshape(depth, Q_LORA, -1)
    pe = w_uq[..., MLA_NOPE:]
    pe_sw = jnp.concatenate([pe[..., half:], pe[..., :half]], axis=-1)
    wq = jnp.concatenate([nope, pe.reshape(depth, Q_LORA, -1), pe_sw.reshape(depth, Q_LORA, -1)],
                         axis=2).astype(BF16)
    wuk = jnp.transpose(w_uk, (0, 2, 3, 1)).astype(BF16)
    wuv = jnp.transpose(w_uv, (0, 2, 1, 3)).astype(BF16)
    return wp_t, wgk, wq, wuk, wuv


def _rope_tables(pos):
    half = MLA_ROPE // 2
    inv = ROPE_BASE ** (-jnp.arange(half, dtype=F32) / half)
    ang = pos.astype(F32)[:, None] * inv[None, :]
    cos, sin = jnp.cos(ang), jnp.sin(ang)
    return jnp.tile(cos, (1, LANES // half)), jnp.tile(jnp.concatenate([-sin, sin], axis=1), (1, LANES // MLA_ROPE))


def _forward(x_prompt, x_sample, cache_ckv, cache_kpe, state_gla, page_table, w_in, w_gk, b_gk, gla_norm_g,
             q_norm_g, w_uq, kv_norm_g, w_uk, w_uv, w_o, ln1_g, ln1_b, w_ffn_gate, w_ffn_up, w_ffn_down,
             ln2_g, ln2_b):
    batch, seq, d = x_prompt.shape
    dec_b, t_new, _ = x_sample.shape
    depth = w_in.shape[0]
    n_p, n_s = batch * seq, dec_b * t_new
    past = page_table.shape[1] * PAGE_SIZE
    alpha = (2.0 * depth) ** 0.25
    assert n_p % PAGE_SIZE == 0 and n_s % PAGE_SIZE == 0 and PAGE_SIZE % t_new == 0

    pos = jnp.concatenate([jnp.tile(jnp.arange(seq, dtype=jnp.int32), batch),
                           jnp.tile(past + jnp.arange(t_new, dtype=jnp.int32), dec_b)])
    cos, sin = _rope_tables(pos)
    cache_kpe_t = jnp.swapaxes(cache_kpe, 2, 3)

    wp_t, wgk, wq, wuk, wuv = _prep_weights(w_in, w_gk, w_uq, w_uk, w_uv)
    wo, wd = w_o.astype(BF16), w_ffn_down.astype(BF16)
    wg, wu = w_ffn_gate, w_ffn_up
    row = lambda v: v.reshape(depth, 1, -1)
    bgk, gn, qg, kg = row(b_gk), row(gla_norm_g), row(q_norm_g), row(kv_norm_g)
    g1, b1, g2, b2 = row(ln1_g), row(ln1_b), row(ln2_g), row(ln2_b)

    x = jnp.concatenate([x_prompt.reshape(n_p, d), x_sample.reshape(n_s, d)], axis=0)
    xb = x.astype(BF16)
    outs = [[] for _ in range(6)]
    seqs_per_blk = PAGE_SIZE // t_new
    for l in range(depth):
        p = _matmul_nt(xb, wp_t, l, F32, 1024, 512, f"proj{l}")
        og_p, s_p = _gla_prompt(p, wgk, bgk, gn, l, batch, seq, f"gla_prompt{l}")
        og_s, s_s = _gla_sample(p, wgk, bgk, gn, state_gla, l, n_p, dec_b, t_new, f"gla_sample{l}")
        qt, ckv, kpe, kcat, vt = _mla_prep(p, qg, kg, wq, wuk, l, cos, sin, f"mla_prep{l}")

        om_p = _attn_prompt(qt, kcat, vt, wuv, l, batch, seq, f"attn_prompt{l}")
        q_s = qt[n_p // PAGE_SIZE:].reshape(-1, QK_PAD, MLA_HEADS, seqs_per_blk, t_new)
        q_s = q_s.transpose(0, 3, 1, 2, 4).reshape(dec_b, QK_PAD, MLA_HEADS * t_new)
        ckv_s = ckv[n_p:].reshape(dec_b, t_new, KV_LORA)
        kpe_s = kpe[n_p:].reshape(dec_b, t_new, MLA_ROPE)
        ol_s = _attn_sample(page_table, q_s, cache_ckv, cache_kpe_t, ckv_s, kpe_s, l, f"attn_sample{l}")
        ol_s = ol_s.reshape(dec_b, MLA_HEADS, t_new, KV_LORA).transpose(1, 0, 2, 3)
        om_s = _head_out(ol_s.reshape(MLA_HEADS, n_s, KV_LORA), wuv, l, f"head_out{l}")

        o_gla = jnp.concatenate([og_p, og_s], axis=0)
        o_mla = jnp.concatenate([om_p, om_s], axis=0)
        h = _matmul2(o_gla, o_mla, wo, l, 1024, 512, f"out_proj{l}")
        x1, x1b = _res_ln(x, h, g1, b1, l, alpha, 0, n_p + n_s, f"ln1_{l}")
        hh = _swiglu_up(x1b, wg, wu, l, 256, 1024, f"ffn_up{l}")
        f = _matmul(hh, wd, l, F32, 512, 256, f"ffn_down{l}")
        if l + 1 < depth:
            x, xb = _res_ln(x1, f, g2, b2, l, alpha, 0, n_p + n_s, f"ln2_{l}")
        else:
            y_p, _ = _res_ln(x1, f, g2, b2, l, alpha, 0, n_p, f"ln2_{l}p")
            y_s, _ = _res_ln(x1, f, g2, b2, l, alpha, n_p, n_s, f"ln2_{l}s")

        outs[0].append(ckv[:n_p].reshape(batch, seq, KV_LORA))
        outs[1].append(kpe[:n_p].reshape(batch, seq, MLA_ROPE))
        outs[2].append(s_p)
        outs[3].append(ckv_s)
        outs[4].append(kpe_s)
        outs[5].append(s_s)

    return (y_p.reshape(batch, seq, d), y_s.reshape(dec_b, t_new, d)) + tuple(jnp.stack(o) for o in outs)


def kernel(x_prompt, x_sample, cache_ckv, cache_kpe, state_gla, page_table, w_in, w_gk, b_gk, gla_norm_g, q_norm_g, w_uq, kv_norm_g, w_uk, w_uv, w_o, ln1_g, ln1_b, w_ffn_gate, w_ffn_up, w_ffn_down, ln2_g, ln2_b):
    return _forward(x_prompt, x_sample, cache_ckv, cache_kpe, state_gla, page_table, w_in, w_gk, b_gk,
                    gla_norm_g, q_norm_g, w_uq, kv_norm_g, w_uk, w_uv, w_o, ln1_g, ln1_b, w_ffn_gate,
                    w_ffn_up, w_ffn_down, ln2_g, ln2_b)
```

```python
import functools

import jax
import jax.numpy as jnp
from jax import lax
from jax.experimental import pallas as pl
from jax.experimental.pallas import tpu as pltpu

GLA_HEADS = 8
GLA_DK = 128
GLA_DV = 256
GLA_GATE_RANK = 16
GLA_GATE_NORM = 16.0
MLA_HEADS = 16
MLA_NOPE = 128
MLA_ROPE = 64
MLA_V = 128
Q_LORA = 896
KV_LORA = 512
MLA_SCALE = (MLA_NOPE + MLA_ROPE) ** -0.5
ROPE_BASE = 10000.0
PAGE_SIZE = 128
NEG = -1e30
LN_EPS = 1e-5
RMS_EPS = 1e-6

LANES = 128
GLA_SUB = 32
GLA_TILE = 128
GLA_HEAD_GROUP = 4
QK_PAD = KV_LORA + LANES
PAGES_PER_STEP = 32
PAGE_SLOTS = 3
ATTN_HEAD_GROUP = 16
ATTN_TK = 512
VMEM_LIMIT = 56 * 1024 * 1024

P_CQ = 0
P_CKV = 896
P_KPE = 1408
P_GLOW = 1536
P_HEAD = 2048
P_GV = 2048
P_GR = 4096
P_GQ = 6144
P_GK = 7168
P_COLS = 8192

BF16 = jnp.bfloat16
F32 = jnp.float32


def _cparams(n_axes):
    return pltpu.CompilerParams(dimension_semantics=("arbitrary",) * n_axes,
                                vmem_limit_bytes=VMEM_LIMIT)


def _pick(n, pref):
    if n <= pref:
        return n
    t = pref
    while n % t:
        t //= 2
    return t


def _bdot(a, b):
    return jnp.dot(a.astype(BF16), b.astype(BF16), preferred_element_type=F32)


def _bdot_nt(a, b):
    return lax.dot_general(a.astype(BF16), b.astype(BF16), (((1,), (1,)), ((), ())),
                           preferred_element_type=F32)


def _mm_kernel(x_ref, w_ref, o_ref):
    o_ref[...] = jnp.dot(x_ref[...], w_ref[...], preferred_element_type=F32).astype(o_ref.dtype)


def _matmul(x, w, layer, out_dtype, tm, tn, name):
    m, k = x.shape
    n = w.shape[2]
    tm, tn = _pick(m, tm), _pick(n, tn)
    return pl.pallas_call(
        _mm_kernel,
        grid=(m // tm, n // tn),
        in_specs=[pl.BlockSpec((tm, k), lambda i, j: (i, 0)),
                  pl.BlockSpec((None, k, tn), lambda i, j: (layer, 0, j))],
        out_specs=pl.BlockSpec((tm, tn), lambda i, j: (i, j)),
        out_shape=jax.ShapeDtypeStruct((m, n), out_dtype),
        compiler_params=_cparams(2),
        name=name,
    )(x, w)


def _mm_nt_kernel(x_ref, w_ref, o_ref):
    o_ref[...] = lax.dot_general(x_ref[...], w_ref[...], (((1,), (1,)), ((), ())),
                                 preferred_element_type=F32).astype(o_ref.dtype)


def _matmul_nt(x, w_t, layer, out_dtype, tm, tn, name):
    m, k = x.shape
    n = w_t.shape[1]
    tm, tn = _pick(m, tm), _pick(n, tn)
    return pl.pallas_call(
        _mm_nt_kernel,
        grid=(m // tm, n // tn),
        in_specs=[pl.BlockSpec((tm, k), lambda i, j: (i, 0)),
                  pl.BlockSpec((None, tn, k), lambda i, j: (layer, j, 0))],
        out_specs=pl.BlockSpec((tm, tn), lambda i, j: (i, j)),
        out_shape=jax.ShapeDtypeStruct((m, n), out_dtype),
        compiler_params=_cparams(2),
        name=name,
    )(x, w_t)


def _mm2_kernel(a_ref, b_ref, wa_ref, wb_ref, o_ref):
    acc = jnp.dot(a_ref[...], wa_ref[...], preferred_element_type=F32)
    acc = acc + jnp.dot(b_ref[...], wb_ref[...], preferred_element_type=F32)
    o_ref[...] = acc


def _mm2_split_kernel(ap_ref, as_ref, bp_ref, bs_ref, wa_ref, wb_ref, o_ref, *, n_prompt_blocks):
    i = pl.program_id(0)

    @pl.when(i < n_prompt_blocks)
    def _():
        _mm2_kernel(ap_ref, bp_ref, wa_ref, wb_ref, o_ref)

    @pl.when(i >= n_prompt_blocks)
    def _():
        _mm2_kernel(as_ref, bs_ref, wa_ref, wb_ref, o_ref)


def _matmul2(a_p, a_s, b_p, b_s, w, layer, tm, tn, name):
    (mp, ka), ms = a_p.shape, a_s.shape[0]
    n = w.shape[2]
    tm = _pick(ms, tm)
    assert mp % tm == 0 and ms % tm == 0 and w.shape[1] == 2 * ka
    tn = _pick(n, tn)
    npb = mp // tm
    p_map = lambda i, j: (jnp.minimum(i, npb - 1), 0)
    s_map = lambda i, j: (jnp.maximum(i - npb, 0), 0)
    return pl.pallas_call(
        functools.partial(_mm2_split_kernel, n_prompt_blocks=npb),
        grid=((mp + ms) // tm, n // tn),
        in_specs=[pl.BlockSpec((tm, ka), p_map), pl.BlockSpec((tm, ka), s_map),
                  pl.BlockSpec((tm, ka), p_map), pl.BlockSpec((tm, ka), s_map),
                  pl.BlockSpec((None, ka, tn), lambda i, j: (layer, 0, j)),
                  pl.BlockSpec((None, ka, tn), lambda i, j: (layer, 1, j))],
        out_specs=pl.BlockSpec((tm, tn), lambda i, j: (i, j)),
        out_shape=jax.ShapeDtypeStruct((mp + ms, n), F32),
        compiler_params=_cparams(2),
        name=name,
    )(a_p, a_s, b_p, b_s, w, w)


def _swiglu_kernel(x_ref, wg_ref, wu_ref, o_ref, wgb_ref, wub_ref):
    @pl.when(pl.program_id(1) == 0)
    def _():
        wgb_ref[...] = wg_ref[...].astype(BF16)
        wub_ref[...] = wu_ref[...].astype(BF16)

    x = x_ref[...]
    g = jnp.dot(x, wgb_ref[...], preferred_element_type=F32)
    u = jnp.dot(x, wub_ref[...], preferred_element_type=F32)
    o_ref[...] = (g * (1.0 / (1.0 + jnp.exp(-g))) * u).astype(o_ref.dtype)


def _swiglu_up(x, w_gate, w_up, layer, tf, tm, name):
    m, k = x.shape
    d_ff = w_gate.shape[2]
    tm = _pick(m, tm)
    w_spec = pl.BlockSpec((None, k, tf), lambda j, i: (layer, 0, j))
    return pl.pallas_call(
        _swiglu_kernel,
        grid=(d_ff // tf, m // tm),
        in_specs=[pl.BlockSpec((tm, k), lambda j, i: (i, 0)), w_spec, w_spec],
        out_specs=pl.BlockSpec((tm, tf), lambda j, i: (i, j)),
        out_shape=jax.ShapeDtypeStruct((m, d_ff), BF16),
        scratch_shapes=[pltpu.VMEM((k, tf), BF16), pltpu.VMEM((k, tf), BF16)],
        compiler_params=_cparams(2),
        name=name,
    )(x, w_gate, w_up)


def _ln_kernel(x_ref, h_ref, g_ref, b_ref, o_ref, ob_ref, *, alpha):
    y = alpha * x_ref[...] + h_ref[...]
    mu = jnp.mean(y, axis=-1, keepdims=True)
    d = y - mu
    var = jnp.mean(d * d, axis=-1, keepdims=True)
    o = d * lax.rsqrt(var + LN_EPS) * g_ref[...] + b_ref[...]
    o_ref[...] = o
    ob_ref[...] = o.astype(BF16)


def _res_ln(x, h, g, b, layer, alpha, row0, rows, name):
    d = x.shape[1]
    tm = _pick(rows, 256)
    assert row0 % tm == 0
    off = row0 // tm
    return pl.pallas_call(
        functools.partial(_ln_kernel, alpha=alpha),
        grid=(rows // tm,),
        in_specs=[pl.BlockSpec((tm, d), lambda i: (i + off, 0)),
                  pl.BlockSpec((tm, d), lambda i: (i + off, 0)),
                  pl.BlockSpec((None, 1, d), lambda i: (layer, 0, 0)),
                  pl.BlockSpec((None, 1, d), lambda i: (layer, 0, 0))],
        out_specs=[pl.BlockSpec((tm, d), lambda i: (i, 0)),
                   pl.BlockSpec((tm, d), lambda i: (i, 0))],
        out_shape=[jax.ShapeDtypeStruct((rows, d), F32),
                   jax.ShapeDtypeStruct((rows, d), BF16)],
        compiler_params=_cparams(1),
        name=name,
    )(x, h, g, b)


def _log_sigmoid(z):
    return jnp.minimum(z, 0.0) - jnp.log(1.0 + jnp.exp(-jnp.abs(z)))


def _split3(x):
    hi = x.astype(BF16)
    r1 = x - hi.astype(F32)
    mid = r1.astype(BF16)
    lo = (r1 - mid.astype(F32)).astype(BF16)
    return jnp.concatenate([hi, mid, lo], axis=1)


def _sum3(y, w):
    return y[:, :w] + y[:, w:2 * w] + y[:, 2 * w:3 * w]


def _iota2(shape):
    return lax.broadcasted_iota(jnp.int32, shape, 0), lax.broadcasted_iota(jnp.int32, shape, 1)


def _idiv(x, n):
    return x >> (n.bit_length() - 1) if n & (n - 1) == 0 else x // n


def _imod(x, n):
    return x & (n - 1) if n & (n - 1) == 0 else x % n


def _ones_where(mask):
    return jnp.where(mask, 1.0, 0.0).astype(BF16)


def _log_gate(gl, wgk, bgk):
    z = jnp.dot(gl.astype(BF16), wgk, preferred_element_type=F32) + bgk
    return _log_sigmoid(z) / GLA_GATE_NORM


def _gla_finish(o, gn, gr):
    ms = jnp.mean(o * o, axis=-1, keepdims=True)
    on = o * lax.rsqrt(ms + RMS_EPS) * gn
    return on * (gr * (1.0 / (1.0 + jnp.exp(-gr))))


def _rows_bcast(x, row, n):
    return jnp.broadcast_to(x[row:row + 1, :], (n, x.shape[1]))


def _gla_prompt_kernel(q_ref, k_ref, v_ref, gl_ref, gr_ref, wgk_ref, bgk_ref, gn_ref,
                       o_ref, sfin_ref, st_ref, *, n_tiles):
    t_blk = pl.program_id(2)
    dk, dv = GLA_DK, GLA_DV

    @pl.when(t_blk == 0)
    def _():
        st_ref[...] = jnp.zeros_like(st_ref)

    n = GLA_TILE
    n_sub = n // GLA_SUB
    r, c = _iota2((n, n))
    tri = _ones_where(c <= r)
    diag_mask = (_idiv(r, GLA_SUB) == _idiv(c, GLA_SUB)) & (c <= r)
    c_sub = lax.broadcasted_iota(jnp.int32, (GLA_SUB, n), 1)
    scale = GLA_DK ** -0.5

    for t, u in [(t, u) for t in range(n_tiles) for u in range(GLA_HEAD_GROUP)]:
        rows = pl.ds(t * n, n)
        kc = slice(u * dk, (u + 1) * dk)
        vc = slice(u * dv, (u + 1) * dv)
        qs = q_ref[rows, kc] * scale
        k = k_ref[rows, kc]
        v = v_ref[rows, vc]
        la = _log_gate(gl_ref[rows, :], wgk_ref[:, kc], bgk_ref[:, kc])
        g = _sum3(jnp.dot(tri, _split3(la), preferred_element_type=F32), GLA_DK)
        g_last = g[n - 1:n, :]

        g_mid = jnp.concatenate(
            [_rows_bcast(g, s * GLA_SUB + GLA_SUB // 2 - 1, GLA_SUB) for s in range(n_sub)], axis=0)
        g_start = jnp.concatenate(
            [jnp.zeros((GLA_SUB, GLA_DK), F32)]
            + [_rows_bcast(g, s * GLA_SUB - 1, GLA_SUB) for s in range(1, n_sub)], axis=0)

        att = jnp.where(diag_mask, _bdot_nt(qs * jnp.exp(g - g_mid), k * jnp.exp(g_mid - g)), 0.0)
        q_off = qs * jnp.exp(g - g_start)
        blocks = [att[:GLA_SUB, :]]
        for s in range(1, n_sub):
            k_off = k * jnp.exp(jnp.minimum(g[s * GLA_SUB - 1:s * GLA_SUB, :] - g, 0.0))
            a = _bdot_nt(q_off[s * GLA_SUB:(s + 1) * GLA_SUB, :], k_off)
            blocks.append(jnp.where(c_sub < s * GLA_SUB, a, att[s * GLA_SUB:(s + 1) * GLA_SUB, :]))
        att = jnp.concatenate(blocks, axis=0)

        st = st_ref[u]
        o = _bdot(att, v) + _bdot_nt(qs * jnp.exp(g), st)
        k_tail = k * jnp.exp(g_last - g)
        st_ref[u] = st * jnp.exp(g_last) + _bdot(v.T, k_tail)
        o_ref[rows, vc] = _gla_finish(o, gn_ref[...], gr_ref[rows, vc]).astype(o_ref.dtype)

    @pl.when(t_blk == pl.num_programs(2) - 1)
    def _():
        for u in range(GLA_HEAD_GROUP):
            sfin_ref[u] = st_ref[u].T


def _gla_prompt(p, wgk, bgk, gn, layer, batch, seq, name):
    ct = _pick(seq, 512)
    n_t = seq // ct
    hg = GLA_HEAD_GROUP
    kw, vw = hg * GLA_DK, hg * GLA_DV
    row = lambda b, h, t: b * n_t + t
    return pl.pallas_call(
        functools.partial(_gla_prompt_kernel, n_tiles=ct // GLA_TILE),
        grid=(batch, GLA_HEADS // hg, n_t),
        in_specs=[pl.BlockSpec((ct, kw), lambda b, h, t: (row(b, h, t), P_GQ // kw + h)),
                  pl.BlockSpec((ct, kw), lambda b, h, t: (row(b, h, t), P_GK // kw + h)),
                  pl.BlockSpec((ct, vw), lambda b, h, t: (row(b, h, t), P_GV // vw + h)),
                  pl.BlockSpec((ct, LANES), lambda b, h, t: (row(b, h, t), P_GLOW // LANES)),
                  pl.BlockSpec((ct, vw), lambda b, h, t: (row(b, h, t), P_GR // vw + h)),
                  pl.BlockSpec((None, LANES, kw), lambda b, h, t: (layer, 0, h)),
                  pl.BlockSpec((None, 1, kw), lambda b, h, t: (layer, 0, h)),
                  pl.BlockSpec((None, 1, GLA_DV), lambda b, h, t: (layer, 0, 0))],
        out_specs=[pl.BlockSpec((ct, vw), lambda b, h, t: (row(b, h, t), h)),
                   pl.BlockSpec((None, hg, GLA_DK, GLA_DV), lambda b, h, t: (b, h, 0, 0))],
        out_shape=[jax.ShapeDtypeStruct((batch * seq, GLA_HEADS * GLA_DV), BF16),
                   jax.ShapeDtypeStruct((batch, GLA_HEADS, GLA_DK, GLA_DV), F32)],
        scratch_shapes=[pltpu.VMEM((hg, GLA_DV, GLA_DK), F32)],
        compiler_params=_cparams(3),
        name=name,
    )(p, p, p, p, p, wgk, bgk, gn)


def _gla_sample_kernel(q_ref, k_ref, v_ref, gl_ref, gr_ref, wgk_ref, bgk_ref, gn_ref, s0_ref,
                       o_ref, sfin_ref, oi_ref, bl_ref, kt_ref, vt_ref, qd_ref, *, t_new, n_b):
    n = n_b * t_new
    r, c = _iota2((n, n))
    same = _idiv(r, t_new) == _idiv(c, t_new)
    causal = same & (c <= r)
    tri = _ones_where(causal)
    ones = _ones_where(same)
    scale = GLA_DK ** -0.5

    qs = q_ref[...] * scale
    k = k_ref[...]
    v = v_ref[...]
    la = _log_gate(gl_ref[...], wgk_ref[...], bgk_ref[...])
    la3 = _split3(la)
    b = _sum3(jnp.dot(tri, la3, preferred_element_type=F32), GLA_DK)
    b_last = _sum3(jnp.dot(ones, la3, preferred_element_type=F32), GLA_DK)
    q_dec = qs * jnp.exp(b)
    att = jnp.where(causal, _bdot_nt(q_dec, k * jnp.exp(-b)), 0.0)
    o_intra = _bdot(att, v)

    bl_ref[...] = b_last
    kt_ref[...] = (k * jnp.exp(b_last - b)).astype(BF16)
    vt_ref[...] = v.T.astype(BF16)
    qd_ref[...] = q_dec.astype(BF16)
    oi_ref[...] = jnp.zeros_like(oi_ref)
    row_seq = _idiv(lax.broadcasted_iota(jnp.int32, (n, 1), 0), t_new)

    def body(i, carry):
        st = s0_ref[i].T
        mine = row_seq == i
        o_full = lax.dot_general(qd_ref[...], st.astype(BF16), (((1,), (1,)), ((), ())),
                                 preferred_element_type=F32)
        oi_ref[...] += jnp.where(mine, o_full, 0.0)
        decay = jnp.exp(bl_ref[pl.ds(pl.multiple_of(i * t_new, t_new), t_new), :][0:1, :])
        kt = jnp.where(mine, kt_ref[...], jnp.zeros_like(kt_ref))
        st_new = st * decay + jnp.dot(vt_ref[...], kt, preferred_element_type=F32)
        sfin_ref[i] = st_new.T
        return carry

    lax.fori_loop(0, n_b, body, 0, unroll=True)
    o_ref[...] = _gla_finish(oi_ref[...] + o_intra, gn_ref[...], gr_ref[...]).astype(o_ref.dtype)


def _gla_sample(p, wgk, bgk, gn, s0, layer, row0, dec_b, t_new, name):
    n_b = GLA_TILE // t_new
    n = n_b * t_new
    r0 = row0 // n
    w128 = lambda base, h: base // GLA_DK + h
    w256 = lambda base, h: base // GLA_DV + h
    return pl.pallas_call(
        functools.partial(_gla_sample_kernel, t_new=t_new, n_b=n_b),
        grid=(dec_b // n_b, GLA_HEADS),
        in_specs=[pl.BlockSpec((n, GLA_DK), lambda i, h: (r0 + i, w128(P_GQ, h))),
                  pl.BlockSpec((n, GLA_DK), lambda i, h: (r0 + i, w128(P_GK, h))),
                  pl.BlockSpec((n, GLA_DV), lambda i, h: (r0 + i, w256(P_GV, h))),
                  pl.BlockSpec((n, LANES), lambda i, h: (r0 + i, P_GLOW // LANES)),
                  pl.BlockSpec((n, GLA_DV), lambda i, h: (r0 + i, w256(P_GR, h))),
                  pl.BlockSpec((None, LANES, GLA_DK), lambda i, h: (layer, 0, h)),
                  pl.BlockSpec((None, 1, GLA_DK), lambda i, h: (layer, 0, h)),
                  pl.BlockSpec((None, 1, GLA_DV), lambda i, h: (layer, 0, 0)),
                  pl.BlockSpec((None, n_b, None, GLA_DK, GLA_DV), lambda i, h: (layer, i, h, 0, 0))],
        out_specs=[pl.BlockSpec((n, GLA_DV), lambda i, h: (i, h)),
                   pl.BlockSpec((n_b, None, GLA_DK, GLA_DV), lambda i, h: (i, h, 0, 0))],
        out_shape=[jax.ShapeDtypeStruct((dec_b * t_new, GLA_HEADS * GLA_DV), BF16),
                   jax.ShapeDtypeStruct(s0.shape[1:], F32)],
        scratch_shapes=[pltpu.VMEM((n, GLA_DV), F32),
                        pltpu.VMEM((n, GLA_DK), F32),
                        pltpu.VMEM((n, GLA_DK), BF16),
                        pltpu.VMEM((GLA_DV, n), BF16),
                        pltpu.VMEM((n, GLA_DK), BF16)],
        compiler_params=_cparams(2),
        name=name,
    )(p, p, p, p, p, wgk, bgk, gn, s0)


def _rms(x, g):
    return x * lax.rsqrt(jnp.mean(x * x, axis=-1, keepdims=True) + RMS_EPS) * g


def _mla_prep_kernel(p_ref, qg_ref, kg_ref, wq_ref, wuk_ref, cos_ref, sin_ref,
                     qt_ref, ckv_ref, kpe_ref, kcat_ref, vt_ref):
    tm = p_ref.shape[0]
    nope_w = MLA_HEADS * MLA_NOPE
    rope_w = MLA_HEADS * MLA_ROPE
    cq = p_ref[:, P_CQ:P_CQ + Q_LORA]
    qa = jnp.dot(_rms(cq, qg_ref[...]).astype(BF16), wq_ref[...], preferred_element_type=F32)
    cos = cos_ref[...]
    sin = sin_ref[...]
    low = lax.broadcasted_iota(jnp.int32, cos.shape, 1) < MLA_ROPE
    for j in range(MLA_HEADS // 2):
        x = qa[:, nope_w + j * LANES:nope_w + (j + 1) * LANES]
        xs = qa[:, nope_w + rope_w + j * LANES:nope_w + rope_w + (j + 1) * LANES]
        rot = x * cos + xs * sin
        tails = (jnp.where(low, rot, 0.0), jnp.where(low, pltpu.roll(rot, MLA_ROPE, 1), 0.0))
        for u in range(2):
            h = 2 * j + u
            q_lat = jnp.dot(qa[:, h * MLA_NOPE:(h + 1) * MLA_NOPE].astype(BF16), wuk_ref[h],
                            preferred_element_type=F32)
            q_t = jnp.concatenate([q_lat, tails[u]], axis=1).T.astype(BF16)
            for tb in range(tm // PAGE_SIZE):
                qt_ref[tb, :, h * PAGE_SIZE:(h + 1) * PAGE_SIZE] = q_t[:, tb * PAGE_SIZE:(tb + 1) * PAGE_SIZE]

    ckv = _rms(p_ref[:, P_CKV:P_CKV + KV_LORA], kg_ref[...])
    kk = p_ref[:, P_KPE:P_KPE + LANES] * jnp.where(low, cos, sin)
    kpe = kk + pltpu.roll(kk, MLA_ROPE, 1)
    ckv_ref[...] = ckv
    kpe_ref[...] = kpe[:, :MLA_ROPE]
    kcat_ref[:, :KV_LORA] = ckv.astype(BF16)
    kcat_ref[:, KV_LORA:] = jnp.where(low, kpe, 0.0).astype(BF16)
    vt_ref[...] = ckv.T.astype(BF16)


def _mla_prep(p, qg, kg, wq, wuk, layer, cos, sin, name):
    t = p.shape[0]
    tm = _pick(t, 256)
    assert tm % PAGE_SIZE == 0
    nb = tm // PAGE_SIZE
    return pl.pallas_call(
        _mla_prep_kernel,
        grid=(t // tm,),
        in_specs=[pl.BlockSpec((tm, P_HEAD), lambda i: (i, 0)),
                  pl.BlockSpec((None, 1, Q_LORA), lambda i: (layer, 0, 0)),
                  pl.BlockSpec((None, 1, KV_LORA), lambda i: (layer, 0, 0)),
                  pl.BlockSpec((None,) + wq.shape[1:], lambda i: (layer, 0, 0)),
                  pl.BlockSpec((None,) + wuk.shape[1:], lambda i: (layer, 0, 0, 0)),
                  pl.BlockSpec((tm, LANES), lambda i: (i, 0)),
                  pl.BlockSpec((tm, LANES), lambda i: (i, 0))],
        out_specs=[pl.BlockSpec((nb, QK_PAD, MLA_HEADS * PAGE_SIZE), lambda i: (i, 0, 0)),
                   pl.BlockSpec((tm, KV_LORA), lambda i: (i, 0)),
                   pl.BlockSpec((tm, MLA_ROPE), lambda i: (i, 0)),
                   pl.BlockSpec((tm, QK_PAD), lambda i: (i, 0)),
                   pl.BlockSpec((KV_LORA, tm), lambda i: (0, i))],
        out_shape=[jax.ShapeDtypeStruct((t // PAGE_SIZE, QK_PAD, MLA_HEADS * PAGE_SIZE), BF16),
                   jax.ShapeDtypeStruct((t, KV_LORA), F32),
                   jax.ShapeDtypeStruct((t, MLA_ROPE), F32),
                   jax.ShapeDtypeStruct((t, QK_PAD), BF16),
                   jax.ShapeDtypeStruct((KV_LORA, t), BF16)],
        compiler_params=_cparams(1),
        name=name,
    )(p, qg, kg, wq, wuk, cos, sin)


def _attn_prompt_kernel(qt_ref, k_ref, vt_ref, wuv_ref, o_ref, m_ref, l_ref, acc_ref, *, tq, tk):
    qi = pl.program_id(1)
    ki = pl.program_id(2)
    last_k = (qi * tq + tq - 1) // tk
    cols = MLA_HEADS * tq

    @pl.when(ki == 0)
    def _():
        m_ref[...] = jnp.full_like(m_ref, NEG)
        l_ref[...] = jnp.zeros_like(l_ref)
        acc_ref[...] = jnp.zeros_like(acc_ref)

    def step(masked):
        k = k_ref[...]
        vt = vt_ref[...]
        cw = ATTN_HEAD_GROUP * tq
        if masked:
            key, col = _iota2((tk, cw))
            visible = ki * tk + key <= qi * tq + _imod(col, tq)
        for j in range(cols // cw):
            cs = slice(j * cw, (j + 1) * cw)
            s_t = jnp.dot(k, qt_ref[:, cs], preferred_element_type=F32) * MLA_SCALE
            if masked:
                s_t = jnp.where(visible, s_t, NEG)
            m_old = m_ref[:, cs]
            m_new = jnp.maximum(m_old, jnp.max(s_t, axis=0, keepdims=True))
            corr = jnp.exp(m_old - m_new)
            p_t = jnp.exp(s_t - m_new)
            l_ref[:, cs] = l_ref[:, cs] * corr + jnp.sum(p_t, axis=0, keepdims=True)
            acc_ref[:, cs] = acc_ref[:, cs] * corr + jnp.dot(vt, p_t.astype(BF16),
                                                             preferred_element_type=F32)
            m_ref[:, cs] = m_new

    @pl.when((ki * tk + tk - 1 <= qi * tq) & (ki <= last_k))
    def _():
        step(False)

    @pl.when((ki * tk + tk - 1 > qi * tq) & (ki <= last_k))
    def _():
        step(True)

    @pl.when(ki == last_k)
    def _():
        for h in range(MLA_HEADS):
            cs = slice(h * tq, (h + 1) * tq)
            o_t = (acc_ref[:, cs] / l_ref[:, cs]).astype(BF16)
            o_ref[:, h * MLA_V:(h + 1) * MLA_V] = lax.dot_general(
                o_t, wuv_ref[h], (((0,), (0,)), ((), ())), preferred_element_type=F32).astype(o_ref.dtype)


def _attn_prompt(qt, kcat, vt, wuv, layer, batch, seq, name):
    tq = PAGE_SIZE
    tk = _pick(seq, ATTN_TK)
    nq, nk = seq // tq, seq // tk

    def k_blk(b, qi, ki):
        return b * nk + jnp.minimum(ki, (qi * tq + tq - 1) // tk)

    return pl.pallas_call(
        functools.partial(_attn_prompt_kernel, tq=tq, tk=tk),
        grid=(batch, nq, nk),
        in_specs=[pl.BlockSpec((None, QK_PAD, MLA_HEADS * tq), lambda b, qi, ki: (b * nq + qi, 0, 0)),
                  pl.BlockSpec((tk, QK_PAD), lambda b, qi, ki: (k_blk(b, qi, ki), 0)),
                  pl.BlockSpec((KV_LORA, tk), lambda b, qi, ki: (0, k_blk(b, qi, ki))),
                  pl.BlockSpec((None,) + wuv.shape[1:], lambda b, qi, ki: (layer, 0, 0, 0))],
        out_specs=pl.BlockSpec((tq, MLA_HEADS * MLA_V), lambda b, qi, ki: (b * nq + qi, 0)),
        out_shape=jax.ShapeDtypeStruct((batch * seq, MLA_HEADS * MLA_V), BF16),
        scratch_shapes=[pltpu.VMEM((1, MLA_HEADS * tq), F32),
                        pltpu.VMEM((1, MLA_HEADS * tq), F32),
                        pltpu.VMEM((KV_LORA, MLA_HEADS * tq), F32)],
        compiler_params=_cparams(3),
        name=name,
    )(qt, kcat, vt, wuv)


def _col(row_vec, n):
    return jnp.broadcast_to(row_vec, (n, n)).T[:, 0:1]


def _page_copies(pt_ref, ckv_hbm, kpe_hbm, ck_buf, kp_buf, sem, layer, seq, page0, slot, n_pg):
    copies = []
    for i in range(n_pg):
        page = pt_ref[seq, page0 + i]
        copies.append(pltpu.make_async_copy(ckv_hbm.at[layer, page], ck_buf.at[slot, i], sem.at[slot]))
        copies.append(pltpu.make_async_copy(kpe_hbm.at[layer, page], kp_buf.at[slot, i], sem.at[slot]))
    return copies


def _attn_sample_kernel(pt_ref, qt_ref, ckv_hbm, kpe_hbm, cn_ref, kn_ref, o_ref,
                        ck_buf, kp_buf, sem, m_ref, l_ref, acc_ref, *, layer, t_new, n_pg):
    b, g = pl.program_id(0), pl.program_id(1)
    n_g = pl.num_programs(1)
    step = b * n_g + g
    last = pl.num_programs(0) * n_g - 1
    ahead = PAGE_SLOTS - 1
    slot = step % PAGE_SLOTS
    rows = MLA_HEADS * t_new
    tn = (((0,), (0,)), ((), ()))
    copies = functools.partial(_page_copies, pt_ref, ckv_hbm, kpe_hbm, ck_buf, kp_buf, sem, layer)

    def start_step(t):
        for c in copies(t // n_g, (t % n_g) * n_pg, t % PAGE_SLOTS, n_pg):
            c.start()

    for t in range(ahead):
        @pl.when((step == 0) & (t <= last))
        def _():
            start_step(t)

    @pl.when(step + ahead <= last)
    def _():
        start_step(step + ahead)

    @pl.when(g == 0)
    def _():
        m_ref[...] = jnp.full_like(m_ref, NEG)
        l_ref[...] = jnp.zeros_like(l_ref)
        acc_ref[...] = jnp.zeros_like(acc_ref)

    def update(s_t, kv):
        m_old = m_ref[...]
        m_new = jnp.maximum(m_old, jnp.max(s_t, axis=0, keepdims=True))
        corr = jnp.exp(m_old - m_new)
        p_t = jnp.exp(s_t - m_new)
        l_ref[...] = l_ref[...] * corr + jnp.sum(p_t, axis=0, keepdims=True)
        m_ref[...] = m_new
        acc_ref[...] = acc_ref[...] * _col(corr, rows) + jnp.dot(p_t.T.astype(BF16), kv,
                                                                 preferred_element_type=F32)

    for c in copies(b, g * n_pg, slot, n_pg):
        c.wait()
    q_lat = qt_ref[:KV_LORA, :]
    q_pe = qt_ref[KV_LORA:KV_LORA + MLA_ROPE, :]
    kv = jnp.concatenate([ck_buf[slot, i].astype(BF16) for i in range(n_pg)], axis=0)
    s_pe = jnp.concatenate(
        [lax.dot_general(kp_buf[slot, i].astype(BF16), q_pe, tn, preferred_element_type=F32)
         for i in range(n_pg)], axis=0)
    update((jnp.dot(kv, q_lat, preferred_element_type=F32) + s_pe) * MLA_SCALE, kv)

    @pl.when(g == n_g - 1)
    def _():
        zc = jnp.zeros((PAGE_SIZE - t_new, KV_LORA), F32)
        zp = jnp.zeros((PAGE_SIZE - t_new, MLA_ROPE), F32)
        key, qrow = _iota2((PAGE_SIZE, rows))
        causal = key <= _imod(qrow, t_new)
        kv_new = jnp.concatenate([cn_ref[...], zc], axis=0).astype(BF16)
        kp_new = jnp.concatenate([kn_ref[...], zp], axis=0).astype(BF16)
        s_t = (jnp.dot(kv_new, q_lat, preferred_element_type=F32)
               + jnp.dot(kp_new, q_pe, preferred_element_type=F32))
        update(jnp.where(causal, s_t * MLA_SCALE, NEG), kv_new)
        o_ref[...] = (acc_ref[...] / _col(l_ref[...], rows)).astype(o_ref.dtype)


def _attn_sample(page_table, q_t, cache_ckv, cache_kpe_t, ckv_new, kpe_new, layer, name):
    dec_b, _, rows = q_t.shape
    t_new = rows // MLA_HEADS
    n_pages = page_table.shape[1]
    n_pg = _pick(n_pages, PAGES_PER_STEP)
    return pl.pallas_call(
        functools.partial(_attn_sample_kernel, layer=layer, t_new=t_new, n_pg=n_pg),
        grid_spec=pltpu.PrefetchScalarGridSpec(
            num_scalar_prefetch=1,
            grid=(dec_b, n_pages // n_pg),
            in_specs=[pl.BlockSpec((None, QK_PAD, rows), lambda b, g, pt: (b, 0, 0)),
                      pl.BlockSpec(memory_space=pl.ANY),
                      pl.BlockSpec(memory_space=pl.ANY),
                      pl.BlockSpec((None, t_new, KV_LORA), lambda b, g, pt: (b, 0, 0)),
                      pl.BlockSpec((None, t_new, MLA_ROPE), lambda b, g, pt: (b, 0, 0))],
            out_specs=pl.BlockSpec((None, rows, KV_LORA), lambda b, g, pt: (b, 0, 0)),
            scratch_shapes=[pltpu.VMEM((PAGE_SLOTS, n_pg, PAGE_SIZE, KV_LORA), F32),
                            pltpu.VMEM((PAGE_SLOTS, n_pg, MLA_ROPE, PAGE_SIZE), F32),
                            pltpu.SemaphoreType.DMA((PAGE_SLOTS,)),
                            pltpu.VMEM((1, rows), F32),
                            pltpu.VMEM((1, rows), F32),
                            pltpu.VMEM((rows, KV_LORA), F32)]),
        out_shape=jax.ShapeDtypeStruct((dec_b, rows, KV_LORA), BF16),
        compiler_params=_cparams(2),
        name=name,
    )(page_table, q_t, cache_ckv, cache_kpe_t, ckv_new, kpe_new)


def _head_out_kernel(x_ref, w_ref, o_ref):
    o_ref[...] = jnp.dot(x_ref[...], w_ref[...], preferred_element_type=F32).astype(o_ref.dtype)


def _head_out(x, wuv, layer, name):
    h, t, c = x.shape
    return pl.pallas_call(
        _head_out_kernel,
        grid=(h,),
        in_specs=[pl.BlockSpec((None, t, c), lambda i: (i, 0, 0)),
                  pl.BlockSpec((None, None, c, MLA_V), lambda i: (layer, i, 0, 0))],
        out_specs=pl.BlockSpec((t, MLA_V), lambda i: (0, i)),
        out_shape=jax.ShapeDtypeStruct((t, h * MLA_V), BF16),
        compiler_params=_cparams(1),
        name=name,
    )(x, wuv)


def _prep_weights(w_in, w_gk, w_uq, w_uk, w_uv):
    depth = w_in.shape[0]
    wt = jnp.swapaxes(w_in, 1, 2)
    s_gq, s_gk, s_gv, s_gl, s_gr, s_cq, s_ckv, s_kpe = jnp.split(
        wt, [1024, 2048, 4096, 4112, 6160, 7056, 7568], axis=1)
    half = MLA_ROPE // 2
    kpe_sw = jnp.concatenate([s_kpe[:, half:], s_kpe[:, :half]], axis=1)
    head = jnp.concatenate([s_cq, s_ckv, s_kpe, kpe_sw, s_gl], axis=1)
    head = jnp.pad(head, ((0, 0), (0, P_HEAD - head.shape[1]), (0, 0)))
    wp_t = jnp.concatenate([head, s_gv, s_gr, s_gq, s_gk], axis=1).astype(BF16)

    wgk = jnp.pad(w_gk, ((0, 0), (0, LANES - GLA_GATE_RANK), (0, 0))).astype(BF16)
    nope = w_uq[..., :MLA_NOPE].reshape(depth, Q_LORA, -1)
    pe = w_uq[..., MLA_NOPE:]
    pe_sw = jnp.concatenate([pe[..., half:], pe[..., :half]], axis=-1)
    wq = jnp.concatenate([nope, pe.reshape(depth, Q_LORA, -1), pe_sw.reshape(depth, Q_LORA, -1)],
                         axis=2).astype(BF16)
    wuk = jnp.transpose(w_uk, (0, 2, 3, 1)).astype(BF16)
    wuv = jnp.transpose(w_uv, (0, 2, 1, 3)).astype(BF16)
    return wp_t, wgk, wq, wuk, wuv


def _rope_tables(pos):
    half = MLA_ROPE // 2
    inv = ROPE_BASE ** (-jnp.arange(half, dtype=F32) / half)
    ang = pos.astype(F32)[:, None] * inv[None, :]
    cos, sin = jnp.cos(ang), jnp.sin(ang)
    return jnp.tile(cos, (1, LANES // half)), jnp.tile(jnp.concatenate([-sin, sin], axis=1), (1, LANES // MLA_ROPE))


def _forward(x_prompt, x_sample, cache_ckv, cache_kpe, state_gla, page_table, w_in, w_gk, b_gk, gla_norm_g,
             q_norm_g, w_uq, kv_norm_g, w_uk, w_uv, w_o, ln1_g, ln1_b, w_ffn_gate, w_ffn_up, w_ffn_down,
             ln2_g, ln2_b):
    batch, seq, d = x_prompt.shape
    dec_b, t_new, _ = x_sample.shape
    depth = w_in.shape[0]
    n_p, n_s = batch * seq, dec_b * t_new
    past = page_table.shape[1] * PAGE_SIZE
    alpha = (2.0 * depth) ** 0.25
    assert n_p % PAGE_SIZE == 0 and n_s % PAGE_SIZE == 0 and PAGE_SIZE % t_new == 0

    pos = jnp.concatenate([jnp.tile(jnp.arange(seq, dtype=jnp.int32), batch),
                           jnp.tile(past + jnp.arange(t_new, dtype=jnp.int32), dec_b)])
    cos, sin = _rope_tables(pos)
    cache_kpe_t = jnp.swapaxes(cache_kpe, 2, 3)

    wp_t, wgk, wq, wuk, wuv = _prep_weights(w_in, w_gk, w_uq, w_uk, w_uv)
    wo, wd = w_o.astype(BF16), w_ffn_down.astype(BF16)
    wg, wu = w_ffn_gate, w_ffn_up
    row = lambda v: v.reshape(depth, 1, -1)
    bgk, gn, qg, kg = row(b_gk), row(gla_norm_g), row(q_norm_g), row(kv_norm_g)
    g1, b1, g2, b2 = row(ln1_g), row(ln1_b), row(ln2_g), row(ln2_b)

    x = jnp.concatenate([x_prompt.reshape(n_p, d), x_sample.reshape(n_s, d)], axis=0)
    xb = x.astype(BF16)
    outs = [[] for _ in range(6)]
    seqs_per_blk = PAGE_SIZE // t_new
    for l in range(depth):
        p = _matmul_nt(xb, wp_t, l, F32, 1024, 512, f"proj{l}")
        og_p, s_p = _gla_prompt(p, wgk, bgk, gn, l, batch, seq, f"gla_prompt{l}")
        og_s, s_s = _gla_sample(p, wgk, bgk, gn, state_gla, l, n_p, dec_b, t_new, f"gla_sample{l}")
        qt, ckv, kpe, kcat, vt = _mla_prep(p, qg, kg, wq, wuk, l, cos, sin, f"mla_prep{l}")

        om_p = _attn_prompt(qt, kcat, vt, wuv, l, batch, seq, f"attn_prompt{l}")
        q_s = qt[n_p // PAGE_SIZE:].reshape(-1, QK_PAD, MLA_HEADS, seqs_per_blk, t_new)
        q_s = q_s.transpose(0, 3, 1, 2, 4).reshape(dec_b, QK_PAD, MLA_HEADS * t_new)
        ckv_s = ckv[n_p:].reshape(dec_b, t_new, KV_LORA)
        kpe_s = kpe[n_p:].reshape(dec_b, t_new, MLA_ROPE)
        ol_s = _attn_sample(page_table, q_s, cache_ckv, cache_kpe_t, ckv_s, kpe_s, l, f"attn_sample{l}")
        ol_s = ol_s.reshape(dec_b, MLA_HEADS, t_new, KV_LORA).transpose(1, 0, 2, 3)
        om_s = _head_out(ol_s.reshape(MLA_HEADS, n_s, KV_LORA), wuv, l, f"head_out{l}")

        h = _matmul2(og_p, og_s, om_p, om_s, wo, l, 1024, 512, f"out_proj{l}")
        x1, x1b = _res_ln(x, h, g1, b1, l, alpha, 0, n_p + n_s, f"ln1_{l}")
        hh = _swiglu_up(x1b, wg, wu, l, 256, 1024, f"ffn_up{l}")
        f = _matmul(hh, wd, l, F32, 512, 256, f"ffn_down{l}")
        if l + 1 < depth:
            x, xb = _res_ln(x1, f, g2, b2, l, alpha, 0, n_p + n_s, f"ln2_{l}")
        else:
            y_p, _ = _res_ln(x1, f, g2, b2, l, alpha, 0, n_p, f"ln2_{l}p")
            y_s, _ = _res_ln(x1, f, g2, b2, l, alpha, n_p, n_s, f"ln2_{l}s")

        outs[0].append(ckv[:n_p].reshape(batch, seq, KV_LORA))
        outs[1].append(kpe[:n_p].reshape(batch, seq, MLA_ROPE))
        outs[2].append(s_p)
        outs[3].append(ckv_s)
        outs[4].append(kpe_s)
        outs[5].append(s_s)

    return (y_p.reshape(batch, seq, d), y_s.reshape(dec_b, t_new, d)) + tuple(jnp.stack(o) for o in outs)


def kernel(x_prompt, x_sample, cache_ckv, cache_kpe, state_gla, page_table, w_in, w_gk, b_gk, gla_norm_g, q_norm_g, w_uq, kv_norm_g, w_uk, w_uv, w_o, ln1_g, ln1_b, w_ffn_gate, w_ffn_up, w_ffn_down, ln2_g, ln2_b):
    return _forward(x_prompt, x_sample, cache_ckv, cache_kpe, state_gla, page_table, w_in, w_gk, b_gk,
                    gla_norm_g, q_norm_g, w_uq, kv_norm_g, w_uk, w_uv, w_o, ln1_g, ln1_b, w_ffn_gate,
                    w_ffn_up, w_ffn_down, ln2_g, ln2_b)
```
